```python
import jax, jax.numpy as jnp
from jax import lax
import numpy as np

D_MODEL = 1024
BATCH = 8
SEQ = 2048
DEPTH = 2
DEC_BATCH = 128
DEC_SEQ = 1
PAST_LEN = 16384
PAGE_SIZE = 128

N_MIXERS = 2
N_CONV_LAYERS = (DEPTH + 1) // 2
N_GMLP_LAYERS = DEPTH // 2
CONV_WIDTH = 31
CONV_CTX = CONV_WIDTH - 1
GMLP_CHUNK = 128
GMLP_WIDTH = 2 * D_MODEL
GMLP_GROUPS = 8
GMLP_GROUP_DIM = GMLP_WIDTH // GMLP_GROUPS
D_FF = -(-8 * D_MODEL // (3 * 256)) * 256
EPS = 1e-6

kernel_name = "hybrid_conformer_conv_chunk_gmlp_adaln_step"


def rmsnorm(x, g):
    xf = x.astype(jnp.float32)
    xf = xf * lax.rsqrt(jnp.mean(xf * xf, axis=-1, keepdims=True) + EPS)
    return (xf * g.astype(jnp.float32)).astype(x.dtype)


def layernorm(x, g, b):
    xf = x.astype(jnp.float32)
    mu = jnp.mean(xf, axis=-1, keepdims=True)
    xc = xf - mu
    var = jnp.mean(xc * xc, axis=-1, keepdims=True)
    y = xc * lax.rsqrt(var + EPS) * g.astype(jnp.float32) + b.astype(jnp.float32)
    return y.astype(x.dtype)


def modulate(h, shift, scale):
    return h * (1 + scale[:, None, :]) + shift[:, None, :]


def conv_module(h, ctx, w_pw1, b_pw1, w_dw, b_dw, ln_g, ln_b, w_pw2, b_pw2):
    a = h @ w_pw1 + b_pw1
    z = a[..., :D_MODEL] * jax.nn.sigmoid(a[..., D_MODEL:])
    full = jnp.concatenate([ctx.astype(z.dtype), z], axis=1)
    y = lax.conv_general_dilated(
        full, w_dw[:, None, :].astype(full.dtype), window_strides=(1,), padding='VALID',
        dimension_numbers=('NWC', 'WIO', 'NWC'), feature_group_count=D_MODEL) + b_dw
    y = jax.nn.silu(layernorm(y, ln_g, ln_b))
    out = y @ w_pw2 + b_pw2
    return out, full[:, -CONV_CTX:, :]


def chunk_mix(v, w_s, b_s):
    B, T, _ = v.shape
    w = jnp.tril(w_s).astype(v.dtype)
    if T < GMLP_CHUNK:
        L, n_chunks, vp = T, 1, v
        w, b = w[:, :T, :T], b_s[:, :T]
    else:
        L = GMLP_CHUNK
        n_chunks = -(-T // GMLP_CHUNK)
        vp = jnp.pad(v, ((0, 0), (0, n_chunks * L - T), (0, 0)))
        b = b_s
    vc = vp.reshape(B, n_chunks, L, GMLP_GROUPS, GMLP_GROUP_DIM)
    m = jnp.einsum('gts,bnsge->bntge', w, vc) + b.T.astype(v.dtype)[None, None, :, :, None]
    return m.reshape(B, n_chunks * L, GMLP_WIDTH)[:, :T]


def gmlp_module(h, w_in, b_in, ln_g, ln_b, w_s, b_s, w_out, b_out):
    a = jax.nn.gelu(h @ w_in + b_in)
    u, v = a[..., :GMLP_WIDTH], a[..., GMLP_WIDTH:]
    v = layernorm(v, ln_g, ln_b)
    mixed = chunk_mix(v, w_s, b_s)
    return (u * mixed) @ w_out + b_out, v


def swiglu(h, w_gate, w_up, w_down):
    return (jax.nn.silu(h @ w_gate) * (h @ w_up)) @ w_down


def trunk(x, c, conv_ctx, w_ada, b_ada, norm_mix_g, norm_ffn_g, final_norm_g,
          conv_w_pw1, conv_b_pw1, conv_w_dw, conv_b_dw, conv_ln_g, conv_ln_b, conv_w_pw2, conv_b_pw2,
          gmlp_w_in, gmlp_b_in, gmlp_ln_g, gmlp_ln_b, gmlp_w_s, gmlp_b_s, gmlp_w_out, gmlp_b_out,
          ffn_w_gate, ffn_w_up, ffn_w_down):
    conv_new, gmlp_v = [], []
    cs = jax.nn.silu(c)
    for i in range(DEPTH):
        mod = cs @ w_ada[i] + b_ada[i]
        sh1, sc1, g1, sh2, sc2, g2 = jnp.split(mod, 6, axis=-1)
        h = modulate(rmsnorm(x, norm_mix_g[i]), sh1, sc1)
        j = i // N_MIXERS
        if i % N_MIXERS == 0:
            out, ctx = conv_module(h, conv_ctx[j], conv_w_pw1[j], conv_b_pw1[j], conv_w_dw[j], conv_b_dw[j],
                                   conv_ln_g[j], conv_ln_b[j], conv_w_pw2[j], conv_b_pw2[j])
            conv_new.append(ctx)
        else:
            out, v = gmlp_module(h, gmlp_w_in[j], gmlp_b_in[j], gmlp_ln_g[j], gmlp_ln_b[j],
                                 gmlp_w_s[j], gmlp_b_s[j], gmlp_w_out[j], gmlp_b_out[j])
            gmlp_v.append(v)
        x = x + g1[:, None, :] * out
        h = modulate(rmsnorm(x, norm_ffn_g[i]), sh2, sc2)
        x = x + g2[:, None, :] * swiglu(h, ffn_w_gate[i], ffn_w_up[i], ffn_w_down[i])
    return rmsnorm(x, final_norm_g), jnp.stack(conv_new), jnp.stack(gmlp_v)


def setup_inputs(seed: int = 0) -> dict:
    key = jax.random.key(seed)
    ks = iter(jax.random.split(key, 40))
    f32 = jnp.float32
    nrm = lambda shape, s: jax.random.normal(next(ks), shape, f32) * s
    gain = lambda shape: 1.0 + nrm(shape, 0.02)
    D, E, F, C, G = D_MODEL, GMLP_WIDTH, D_FF, GMLP_CHUNK, GMLP_GROUPS
    NA, NB = N_CONV_LAYERS, N_GMLP_LAYERS
    return {
        "x_prompt": nrm((BATCH, SEQ, D), 1.0),
        "x_sample": nrm((DEC_BATCH, DEC_SEQ, D), 1.0),
        "c_prompt": nrm((BATCH, D), 1.0),
        "c_sample": nrm((DEC_BATCH, D), 1.0),
        "state_conv": nrm((NA, DEC_BATCH, CONV_CTX, D), 0.5),
        "w_ada": nrm((DEPTH, D, 6 * D), 0.5 * D ** -0.5),
        "b_ada": nrm((DEPTH, 6 * D), 0.02),
        "norm_mix_g": gain((DEPTH, D)),
        "norm_ffn_g": gain((DEPTH, D)),
        "final_norm_g": gain((D,)),
        "conv_w_pw1": nrm((NA, D, 2 * D), D ** -0.5),
        "conv_b_pw1": nrm((NA, 2 * D), 0.02),
        "conv_w_dw": nrm((NA, CONV_WIDTH, D), CONV_WIDTH ** -0.5),
        "conv_b_dw": nrm((NA, D), 0.02),
        "conv_ln_g": gain((NA, D)),
        "conv_ln_b": nrm((NA, D), 0.02),
        "conv_w_pw2": nrm((NA, D, D), D ** -0.5),
        "conv_b_pw2": nrm((NA, D), 0.02),
        "gmlp_w_in": nrm((NB, D, 2 * E), D ** -0.5),
        "gmlp_b_in": nrm((NB, 2 * E), 0.02),
        "gmlp_ln_g": gain((NB, E)),
        "gmlp_ln_b": nrm((NB, E), 0.02),
        "gmlp_w_s": nrm((NB, G, C, C), C ** -0.5),
        "gmlp_b_s": gain((NB, G, C)),
        "gmlp_w_out": nrm((NB, E, D), E ** -0.5),
        "gmlp_b_out": nrm((NB, D), 0.02),
        "ffn_w_gate": nrm((DEPTH, D, F), D ** -0.5),
        "ffn_w_up": nrm((DEPTH, D, F), D ** -0.5),
        "ffn_w_down": nrm((DEPTH, F, D), F ** -0.5),
    }


def reference(x_prompt, x_sample, c_prompt, c_sample, state_conv, w_ada, b_ada, norm_mix_g, norm_ffn_g,
              final_norm_g, conv_w_pw1, conv_b_pw1, conv_w_dw, conv_b_dw, conv_ln_g, conv_ln_b, conv_w_pw2,
              conv_b_pw2, gmlp_w_in, gmlp_b_in, gmlp_ln_g, gmlp_ln_b, gmlp_w_s, gmlp_b_s, gmlp_w_out,
              gmlp_b_out, ffn_w_gate, ffn_w_up, ffn_w_down):
    params = (w_ada, b_ada, norm_mix_g, norm_ffn_g, final_norm_g,
              conv_w_pw1, conv_b_pw1, conv_w_dw, conv_b_dw, conv_ln_g, conv_ln_b, conv_w_pw2, conv_b_pw2,
              gmlp_w_in, gmlp_b_in, gmlp_ln_g, gmlp_ln_b, gmlp_w_s, gmlp_b_s, gmlp_w_out, gmlp_b_out,
              ffn_w_gate, ffn_w_up, ffn_w_down)
    zero_ctx = jnp.zeros((N_CONV_LAYERS, x_prompt.shape[0], CONV_CTX, D_MODEL), x_prompt.dtype)
    y_prompt, conv_state_prompt, _ = trunk(x_prompt, c_prompt, zero_ctx, *params)
    y_sample, conv_state_sample, gmlp_v_sample = trunk(x_sample, c_sample, state_conv, *params)
    return (y_prompt, y_sample, conv_state_prompt, conv_state_sample, gmlp_v_sample)
```

```python
import functools

import jax
import jax.numpy as jnp
from jax.experimental import pallas as pl
from jax.experimental.pallas import tpu as pltpu

D_MODEL = 1024
CONV_WIDTH = 31
CONV_CTX = CONV_WIDTH - 1
GMLP_CHUNK = 128
GMLP_WIDTH = 2 * D_MODEL
GMLP_GROUPS = 8
GMLP_GROUP_DIM = GMLP_WIDTH // GMLP_GROUPS
EPS = 1e-6

SUBLANES = 8
CARRY_ROWS = 32
CARRY_SKEW = CARRY_ROWS - CONV_CTX
PROMPT_TILE = 256
ADA_TILE_N = 1536
SAMPLE_BLOCK = 8
VMEM_LIMIT_BYTES = 60 * 1024 * 1024

F32 = jnp.float32
BF16 = jnp.bfloat16


def _dot(a, b):
    return jnp.dot(a.astype(BF16), b, preferred_element_type=F32)


def _rmsnorm(x, g):
    return x * jax.lax.rsqrt(jnp.mean(x * x, axis=-1, keepdims=True) + EPS) * g


def _layernorm(x, g, b):
    mu = jnp.mean(x, axis=-1, keepdims=True)
    xc = x - mu
    var = jnp.mean(xc * xc, axis=-1, keepdims=True)
    return xc * jax.lax.rsqrt(var + EPS) * g + b


def _silu(x):
    return x * jax.nn.sigmoid(x)


def _swiglu(h, wg_ref, wu_ref, wd_ref):
    hb = h.astype(BF16)
    gate = jnp.dot(hb, wg_ref[...], preferred_element_type=F32)
    up = jnp.dot(hb, wu_ref[...], preferred_element_type=F32)
    return _dot(_silu(gate) * up, wd_ref[...])


def _split_mod(mod, rows_are_tokens):
    d = D_MODEL
    if rows_are_tokens:
        return [mod[:, i * d:(i + 1) * d] for i in range(6)]
    return [mod[i:i + 1, :] for i in range(6)]


def _ada_kernel(c_ref, w_ref, b_ref, o_ref):
    cs = _silu(c_ref[...])
    o_ref[...] = _dot(cs, w_ref[...].astype(BF16)) + b_ref[...]


def _ada_call(c_all, w_ada, b_ada):
    depth, d, n = w_ada.shape
    rows = c_all.shape[0]
    return pl.pallas_call(
        _ada_kernel,
        grid=(depth, n // ADA_TILE_N),
        in_specs=[
            pl.BlockSpec((rows, d), lambda i, j: (0, 0)),
            pl.BlockSpec((None, d, ADA_TILE_N), lambda i, j: (i, 0, j)),
            pl.BlockSpec((None, 1, ADA_TILE_N), lambda i, j: (i, 0, j)),
        ],
        out_specs=pl.BlockSpec((None, rows, ADA_TILE_N), lambda i, j: (i, 0, j)),
        out_shape=jax.ShapeDtypeStruct((depth, rows, n), F32),
        compiler_params=pltpu.CompilerParams(dimension_semantics=("arbitrary", "arbitrary")),
        name="adaln_mod",
    )(c_all, w_ada, b_ada.reshape(depth, 1, n))


def _conv_tail(y, x, g1, sh2, sc2, g2, gffn, lng, lnb, wpw2_ref, bpw2, wg_ref, wu_ref, wd_ref):
    y = _silu(_layernorm(y, lng, lnb))
    x = x + g1 * (_dot(y, wpw2_ref[...]) + bpw2)
    h = _rmsnorm(x, gffn) * (1.0 + sc2) + sh2
    return x + g2 * _swiglu(h, wg_ref, wu_ref, wd_ref)


def _l0_prompt_kernel(x_ref, mod_ref, gmix_ref, gffn_ref, wpw1_ref, bpw1_ref, wdw_ref, bdw_ref,
                      lng_ref, lnb_ref, wpw2_ref, bpw2_ref, wg_ref, wu_ref, wd_ref,
                      xo_ref, cs_ref, zbuf):
    d = D_MODEL
    tt = x_ref.shape[0]
    j = pl.program_id(1)

    @pl.when(j == 0)
    def _():
        zbuf[0:CARRY_ROWS, :] = jnp.zeros((CARRY_ROWS, d), F32)

    x = x_ref[...]
    sh1, sc1, g1, sh2, sc2, g2 = _split_mod(mod_ref[...], False)
    h = _rmsnorm(x, gmix_ref[...]) * (1.0 + sc1) + sh1
    a = _dot(h, wpw1_ref[...]) + bpw1_ref[...]
    zbuf[CARRY_ROWS:CARRY_ROWS + tt, :] = a[:, :d] * jax.nn.sigmoid(a[:, d:])

    rb = 32
    ys = []
    for r0 in range(0, tt, rb):
        acc = jnp.zeros((rb, d), F32)
        for k in range(CONV_WIDTH):
            acc = acc + wdw_ref[k:k + 1, :] * zbuf[r0 + CARRY_SKEW + k:r0 + CARRY_SKEW + k + rb, :]
        ys.append(acc)
    y = jnp.concatenate(ys, axis=0) + bdw_ref[...]

    @pl.when(j == pl.num_programs(1) - 1)
    def _():
        cs_ref[...] = zbuf[tt + CARRY_SKEW:tt + CARRY_ROWS, :]

    zbuf[0:CARRY_ROWS, :] = zbuf[tt:tt + CARRY_ROWS, :]

    xo_ref[...] = _conv_tail(y, x, g1, sh2, sc2, g2, gffn_ref[...], lng_ref[...], lnb_ref[...],
                             wpw2_ref, bpw2_ref[...], wg_ref, wu_ref, wd_ref)


def _l0_sample_kernel(x_ref, mod_ref, ctx_ref, gmix_ref, gffn_ref, wpw1_ref, bpw1_ref, wdw_ref, bdw_ref,
                      lng_ref, lnb_ref, wpw2_ref, bpw2_ref, wg_ref, wu_ref, wd_ref,
                      xo_ref, cso_ref, z_scr, y_scr):
    d = D_MODEL
    sb = ctx_ref.shape[0]
    i = pl.program_id(0)

    @pl.when(i == 0)
    def _():
        sh1, sc1 = _split_mod(mod_ref[...], True)[:2]
        h = _rmsnorm(x_ref[...], gmix_ref[...]) * (1.0 + sc1) + sh1
        a = _dot(h, wpw1_ref[...]) + bpw1_ref[...]
        z_scr[...] = a[:, :d] * jax.nn.sigmoid(a[:, d:])

    r0 = pl.multiple_of(i * sb, sb)
    w_ctx = wdw_ref[0:CONV_CTX, :]
    w_new = wdw_ref[CONV_CTX:CONV_WIDTH, :]
    for s in range(sb):
        zs = z_scr[pl.ds(r0 + s, 1), :]
        ctx = ctx_ref[s]
        y_scr[pl.ds(r0 + s, 1), :] = (jnp.sum(ctx * w_ctx, axis=0, keepdims=True) + zs * w_new + bdw_ref[...])
        cso_ref[s, 0:CONV_CTX - 1, :] = ctx[1:CONV_CTX, :]
        cso_ref[s, CONV_CTX - 1:CONV_CTX, :] = zs

    @pl.when(i == pl.num_programs(0) - 1)
    def _():
        _, _, g1, sh2, sc2, g2 = _split_mod(mod_ref[...], True)
        xo_ref[...] = _conv_tail(y_scr[...], x_ref[...], g1, sh2, sc2, g2, gffn_ref[...], lng_ref[...],
                                 lnb_ref[...], wpw2_ref, bpw2_ref[...], wg_ref, wu_ref, wd_ref)


def _gmlp_in(x, sh1, sc1, gmix, win_ref, bin, lng, lnb):
    h = _rmsnorm(x, gmix) * (1.0 + sc1) + sh1
    a = jax.nn.gelu(_dot(h, win_ref[...]) + bin)
    return a[:, :GMLP_WIDTH], _layernorm(a[:, GMLP_WIDTH:], lng, lnb)


def _gmlp_tail(um, x, g1, sh2, sc2, g2, gffn, gfin, wout_ref, bout, wg_ref, wu_ref, wd_ref):
    x = x + g1 * (_dot(um, wout_ref[...]) + bout)
    h = _rmsnorm(x, gffn) * (1.0 + sc2) + sh2
    x = x + g2 * _swiglu(h, wg_ref, wu_ref, wd_ref)
    return _rmsnorm(x, gfin)


def _l1_prompt_kernel(x_ref, mod_ref, gmix_ref, gffn_ref, gfin_ref, win_ref, bin_ref, lng_ref, lnb_ref,
                      ws_ref, bsf_ref, wout_ref, bout_ref, wg_ref, wu_ref, wd_ref, yo_ref, um_scr):
    c, gd = GMLP_CHUNK, GMLP_GROUP_DIM
    tt = x_ref.shape[0]
    x = x_ref[...]
    sh1, sc1, g1, sh2, sc2, g2 = _split_mod(mod_ref[...], False)
    u, v = _gmlp_in(x, sh1, sc1, gmix_ref[...], win_ref, bin_ref[...], lng_ref[...], lnb_ref[...])
    vb = v.astype(BF16)

    row = jax.lax.broadcasted_iota(jnp.int32, (c, c), 0)
    col = jax.lax.broadcasted_iota(jnp.int32, (c, c), 1)
    causal = row >= col
    for g in range(GMLP_GROUPS):
        wt = jnp.where(causal, ws_ref[g], 0.0).astype(BF16)
        bias = bsf_ref[:, g * gd:(g + 1) * gd]
        for r0 in range(0, tt, c):
            m = jnp.dot(wt, vb[r0:r0 + c, g * gd:(g + 1) * gd], preferred_element_type=F32) + bias
            um_scr[r0:r0 + c, g * gd:(g + 1) * gd] = (u[r0:r0 + c, g * gd:(g + 1) * gd] * m).astype(BF16)

    yo_ref[...] = _gmlp_tail(um_scr[...], x, g1, sh2, sc2, g2, gffn_ref[...], gfin_ref[...],
                             wout_ref, bout_ref[...], wg_ref, wu_ref, wd_ref)


def _l1_sample_kernel(x_ref, mod_ref, gmix_ref, gffn_ref, gfin_ref, win_ref, bin_ref, lng_ref, lnb_ref,
                      w00_ref, b0_ref, wout_ref, bout_ref, wg_ref, wu_ref, wd_ref, yo_ref, vo_ref):
    x = x_ref[...]
    sh1, sc1, g1, sh2, sc2, g2 = _split_mod(mod_ref[...], True)
    u, v = _gmlp_in(x, sh1, sc1, gmix_ref[...], win_ref, bin_ref[...], lng_ref[...], lnb_ref[...])
    vo_ref[...] = v
    um = u * (w00_ref[...] * v + b0_ref[...])
    yo_ref[...] = _gmlp_tail(um, x, g1, sh2, sc2, g2, gffn_ref[...], gfin_ref[...],
                             wout_ref, bout_ref[...], wg_ref, wu_ref, wd_ref)


def _resident(shape):
    zeros = (0,) * len(shape)
    return pl.BlockSpec(shape, lambda *_: zeros, pipeline_mode=pl.Buffered(1))


def _row(v):
    return v.reshape(1, -1)


def _params(n_grid):
    return pltpu.CompilerParams(dimension_semantics=("arbitrary",) * n_grid,
                                vmem_limit_bytes=VMEM_LIMIT_BYTES)


def _l0_prompt_call(x, mod, consts):
    b, t, d = x.shape
    tt = PROMPT_TILE
    tile = pl.BlockSpec((None, tt, d), lambda i, j: (i, j, 0))
    return pl.pallas_call(
        _l0_prompt_kernel,
        grid=(b, t // tt),
        in_specs=[tile, pl.BlockSpec((None, 6, d), lambda i, j: (i, 0, 0))] + [_resident(c.shape) for c in consts],
        out_specs=[tile, pl.BlockSpec((None, CONV_CTX, d), lambda i, j: (i, 0, 0))],
        out_shape=[jax.ShapeDtypeStruct((b, t, d), F32), jax.ShapeDtypeStruct((b, CONV_CTX, d), F32)],
        scratch_shapes=[pltpu.VMEM((CARRY_ROWS + tt, d), F32)],
        compiler_params=_params(2),
        name="l0_prompt",
    )(x, mod, *consts)


def _l0_sample_call(x, mod, ctx, consts):
    n, d = x.shape
    sb = SAMPLE_BLOCK
    state = pl.BlockSpec((sb, CONV_CTX, d), lambda i: (i, 0, 0))
    return pl.pallas_call(
        _l0_sample_kernel,
        grid=(n // sb,),
        in_specs=[_resident(x.shape), _resident(mod.shape), state] + [_resident(c.shape) for c in consts],
        out_specs=[pl.BlockSpec((n, d), lambda i: (0, 0)), state],
        out_shape=[jax.ShapeDtypeStruct((n, d), F32), jax.ShapeDtypeStruct(ctx.shape, F32)],
        scratch_shapes=[pltpu.VMEM((n, d), F32), pltpu.VMEM((n, d), F32)],
        compiler_params=_params(1),
        name="l0_sample",
    )(x, mod, ctx, *consts)


def _l1_prompt_call(x, mod, consts):
    b, t, d = x.shape
    tt = PROMPT_TILE
    tile = pl.BlockSpec((None, tt, d), lambda i, j: (i, j, 0))
    return pl.pallas_call(
        _l1_prompt_kernel,
        grid=(b, t // tt),
        in_specs=[tile, pl.BlockSpec((None, 6, d), lambda i, j: (i, 0, 0))] + [_resident(c.shape) for c in consts],
        out_specs=tile,
        out_shape=jax.ShapeDtypeStruct((b, t, d), F32),
        scratch_shapes=[pltpu.VMEM((tt, GMLP_WIDTH), BF16)],
        compiler_params=_params(2),
        name="l1_prompt",
    )(x, mod, *consts)


def _l1_sample_call(x, mod, consts):
    n, d = x.shape
    return pl.pallas_call(
        _l1_sample_kernel,
        grid=(1,),
        in_specs=[_resident(x.shape), _resident(mod.shape)] + [_resident(c.shape) for c in consts],
        out_specs=[pl.BlockSpec((n, d), lambda i: (0, 0)), pl.BlockSpec((n, GMLP_WIDTH), lambda i: (0, 0))],
        out_shape=[jax.ShapeDtypeStruct((n, d), F32), jax.ShapeDtypeStruct((n, GMLP_WIDTH), F32)],
        compiler_params=_params(1),
        name="l1_sample",
    )(x, mod, *consts)


def kernel(x_prompt, x_sample, c_prompt, c_sample, state_conv, w_ada, b_ada, norm_mix_g, norm_ffn_g, final_norm_g, conv_w_pw1, conv_b_pw1, conv_w_dw, conv_b_dw, conv_ln_g, conv_ln_b, conv_w_pw2, conv_b_pw2, gmlp_w_in, gmlp_b_in, gmlp_ln_g, gmlp_ln_b, gmlp_w_s, gmlp_b_s, gmlp_w_out, gmlp_b_out, ffn_w_gate, ffn_w_up, ffn_w_down):
    nb, _, d = x_prompt.shape
    ns = x_sample.shape[0]
    assert x_sample.shape[1] == 1 and state_conv.shape[0] == 1 and gmlp_w_in.shape[0] == 1

    mod = _ada_call(jnp.concatenate([c_prompt, c_sample], axis=0), w_ada, b_ada)
    mod_p = mod[:, :nb].reshape(2, nb, 6, d)
    mod_s = mod[:, nb:]

    ffn = [(ffn_w_gate[i].astype(BF16), ffn_w_up[i].astype(BF16), ffn_w_down[i].astype(BF16)) for i in range(2)]

    l0 = (_row(norm_mix_g[0]), _row(norm_ffn_g[0]), conv_w_pw1[0].astype(BF16), _row(conv_b_pw1[0]),
          conv_w_dw[0], _row(conv_b_dw[0]), _row(conv_ln_g[0]), _row(conv_ln_b[0]),
          conv_w_pw2[0].astype(BF16), _row(conv_b_pw2[0])) + ffn[0]
    l1_head = (_row(norm_mix_g[1]), _row(norm_ffn_g[1]), _row(final_norm_g), gmlp_w_in[0].astype(BF16),
               _row(gmlp_b_in[0]), _row(gmlp_ln_g[0]), _row(gmlp_ln_b[0]))
    l1_tail = (gmlp_w_out[0].astype(BF16), _row(gmlp_b_out[0])) + ffn[1]

    bias_full = jnp.repeat(gmlp_b_s[0].T, GMLP_GROUP_DIM, axis=1)
    w00 = _row(jnp.repeat(gmlp_w_s[0][:, 0, 0], GMLP_GROUP_DIM))
    b0 = _row(jnp.repeat(gmlp_b_s[0][:, 0], GMLP_GROUP_DIM))

    xp, cs_p = _l0_prompt_call(x_prompt, mod_p[0], l0)
    y_prompt = _l1_prompt_call(xp, mod_p[1], l1_head + (gmlp_w_s[0], bias_full) + l1_tail)

    xs, cs_s = _l0_sample_call(x_sample.reshape(ns, d), mod_s[0], state_conv[0], l0)
    y_sample, v_s = _l1_sample_call(xs, mod_s[1], l1_head + (w00, b0) + l1_tail)

    return (y_prompt, y_sample.reshape(ns, 1, d), cs_p[None], cs_s[None], v_s.reshape(1, ns, 1, GMLP_WIDTH))
```

```python
import functools

import jax
import jax.numpy as jnp
from jax.experimental import pallas as pl
from jax.experimental.pallas import tpu as pltpu

D_MODEL = 1024
CONV_WIDTH = 31
CONV_CTX = CONV_WIDTH - 1
GMLP_CHUNK = 128
GMLP_WIDTH = 2 * D_MODEL
GMLP_GROUPS = 8
GMLP_GROUP_DIM = GMLP_WIDTH // GMLP_GROUPS
EPS = 1e-6

SUBLANES = 8
LANES = 128
CARRY_ROWS = 32
CARRY_SKEW = CARRY_ROWS - CONV_CTX
PROMPT_TILE = 512
CONV_ROW_BLOCK = 64
ADA_TILE_N = 1536
SAMPLE_BLOCK = 8
VMEM_LIMIT_BYTES = 60 * 1024 * 1024

F32 = jnp.float32
BF16 = jnp.bfloat16


def _dot(a, b):
    return jnp.dot(a.astype(BF16), b, preferred_element_type=F32)


def _rmsnorm(x, g):
    return x * jax.lax.rsqrt(jnp.mean(x * x, axis=-1, keepdims=True) + EPS) * g


def _layernorm(x, g, b):
    mu = jnp.mean(x, axis=-1, keepdims=True)
    xc = x - mu
    var = jnp.mean(xc * xc, axis=-1, keepdims=True)
    return xc * jax.lax.rsqrt(var + EPS) * g + b


def _silu(x):
    return x * jax.nn.sigmoid(x)


def _modulated_norm(x, g, shift, scale):
    return _rmsnorm(x, g) * (1.0 + scale) + shift


def _swiglu(hb, wg_ref, wu_ref, wd_ref):
    gate = jnp.dot(hb, wg_ref[...], preferred_element_type=F32)
    up = jnp.dot(hb, wu_ref[...], preferred_element_type=F32)
    return _dot(_silu(gate) * up, wd_ref[...])


def _split_mod(mod_ref, row=None):
    d = D_MODEL
    rows = slice(None) if row is None else pl.ds(row, 1)
    return [mod_ref[rows, i * d:(i + 1) * d] for i in range(6)]


def _ada_kernel(cp_ref, cs_ref, w_ref, b_ref, op_ref, os_ref):
    w = w_ref[...].astype(BF16)
    op_ref[...] = _dot(_silu(cp_ref[...]), w) + b_ref[...]
    os_ref[...] = _dot(_silu(cs_ref[...]), w) + b_ref[...]


def _ada_call(c_prompt, c_sample, w_ada, b_ada):
    depth, d, n = w_ada.shape
    col_tile = lambda rows: pl.BlockSpec((None, rows, ADA_TILE_N), lambda i, j: (i, 0, j))
    return pl.pallas_call(
        _ada_kernel,
        grid=(depth, n // ADA_TILE_N),
        in_specs=[
            pl.BlockSpec(c_prompt.shape, lambda i, j: (0, 0)),
            pl.BlockSpec(c_sample.shape, lambda i, j: (0, 0)),
            col_tile(d),
            col_tile(1),
        ],
        out_specs=[col_tile(c_prompt.shape[0]), col_tile(c_sample.shape[0])],
        out_shape=[jax.ShapeDtypeStruct((depth, c_prompt.shape[0], n), F32),
                   jax.ShapeDtypeStruct((depth, c_sample.shape[0], n), F32)],
        compiler_params=pltpu.CompilerParams(dimension_semantics=("arbitrary", "arbitrary")),
        name="adaln_mod",
    )(c_prompt, c_sample, w_ada, b_ada.reshape(depth, 1, n))


def _glu_in(x, sh1, sc1, gmix, wpw1_ref, bpw1):
    d = D_MODEL
    a = _dot(_modulated_norm(x, gmix, sh1, sc1), wpw1_ref[...]) + bpw1
    return a[:, :d] * jax.nn.sigmoid(a[:, d:])


def _conv_out(y, x, g1, lng, lnb, wpw2_ref, bpw2):
    y = _silu(_layernorm(y, lng, lnb))
    return x + g1 * (_dot(y, wpw2_ref[...]) + bpw2)


def _prompt_steps(n_tiles, tiles_per_seq):
    s = pl.program_id(0)
    ta = jnp.minimum(s, n_tiles - 1)
    tb = jnp.maximum(s - 1, 0)
    return s, ta, ta // tiles_per_seq, tb // tiles_per_seq


def _l0_prompt_kernel(x_ref, mod_ref, gmix_ref, gffn_ref, wpw1_ref, bpw1_ref, wdw_ref, bdw_ref,
                      lng_ref, lnb_ref, wpw2_ref, bpw2_ref, wg_ref, wu_ref, wd_ref,
                      xo_ref, cs_ref, zbuf, zsh, ybuf, x1_scr, h2_scr, *, n_tiles, tiles_per_seq):
    d = D_MODEL
    tt = x_ref.shape[0]
    s, ta, seq_a, seq_b = _prompt_steps(n_tiles, tiles_per_seq)

    @pl.when(s == 0)
    def _():
        x1_scr[...] = jnp.zeros(x1_scr.shape, F32)
        h2_scr[...] = jnp.zeros(h2_scr.shape, BF16)

    @pl.when(ta % tiles_per_seq == 0)
    def _():
        zbuf[0:CARRY_ROWS, :] = jnp.zeros((CARRY_ROWS, d), F32)

    x = x_ref[...]
    sh1, sc1, g1, sh2, sc2, _ = _split_mod(mod_ref, seq_a)
    zbuf[CARRY_ROWS:CARRY_ROWS + tt, :] = _glu_in(x, sh1, sc1, gmix_ref[...], wpw1_ref, bpw1_ref[...])

    g2_b = _split_mod(mod_ref, seq_b)[5]
    xo_ref[...] = x1_scr[...] + g2_b * _swiglu(h2_scr[...], wg_ref, wu_ref, wd_ref)

    n_sh = tt + CARRY_ROWS - SUBLANES
    rb = CONV_ROW_BLOCK
    for l0 in range(0, d, LANES):
        lanes = slice(l0, l0 + LANES)
        for sft in range(1, SUBLANES):
            zsh[sft - 1, :, :] = zbuf[sft:sft + n_sh, lanes]
        for r0 in range(0, tt, rb):
            acc = jnp.zeros((rb, LANES), F32)
            for k in range(CONV_WIDTH):
                q, sft = divmod(CARRY_SKEW + k, SUBLANES)
                rows = slice(r0 + q * SUBLANES, r0 + q * SUBLANES + rb)
                win = zbuf[rows, lanes] if sft == 0 else zsh[sft - 1, rows, :]
                acc = acc + wdw_ref[k:k + 1, lanes] * win
            ybuf[r0:r0 + rb, lanes] = acc + bdw_ref[:, lanes]

    x1 = _conv_out(ybuf[...], x, g1, lng_ref[...], lnb_ref[...], wpw2_ref, bpw2_ref[...])
    x1_scr[...] = x1
    h2_scr[...] = _modulated_norm(x1, gffn_ref[...], sh2, sc2).astype(BF16)

    @pl.when(ta % tiles_per_seq == tiles_per_seq - 1)
    def _():
        cs_ref[...] = zbuf[tt + CARRY_SKEW:tt + CARRY_ROWS, :]

    zbuf[0:CARRY_ROWS, :] = zbuf[tt:tt + CARRY_ROWS, :]


def _l0_sample_kernel(x_ref, mod_ref, ctx_ref, gmix_ref, gffn_ref, wpw1_ref, bpw1_ref, wdw_ref, bdw_ref,
                      lng_ref, lnb_ref, wpw2_ref, bpw2_ref, wg_ref, wu_ref, wd_ref,
                      xo_ref, cso_ref, z_scr, y_scr):
    sb = ctx_ref.shape[0]
    i = pl.program_id(0)

    @pl.when(i == 0)
    def _():
        sh1, sc1 = _split_mod(mod_ref)[:2]
        z_scr[...] = _glu_in(x_ref[...], sh1, sc1, gmix_ref[...], wpw1_ref, bpw1_ref[...])

    r0 = pl.multiple_of(i * sb, sb)
    w_ctx = wdw_ref[0:CONV_CTX, :]
    w_new = wdw_ref[CONV_CTX:CONV_WIDTH, :]
    for j in range(sb):
        zs = z_scr[pl.ds(r0 + j, 1), :]
        ctx = ctx_ref[j]
        y_scr[pl.ds(r0 + j, 1), :] = (jnp.sum(ctx * w_ctx, axis=0, keepdims=True) + zs * w_new + bdw_ref[...])
        cso_ref[j, 0:CONV_CTX - 1, :] = ctx[1:CONV_CTX, :]
        cso_ref[j, CONV_CTX - 1:CONV_CTX, :] = zs

    @pl.when(i == pl.num_programs(0) - 1)
    def _():
        _, _, g1, sh2, sc2, g2 = _split_mod(mod_ref)
        x1 = _conv_out(y_scr[...], x_ref[...], g1, lng_ref[...], lnb_ref[...], wpw2_ref, bpw2_ref[...])
        hb = _modulated_norm(x1, gffn_ref[...], sh2, sc2).astype(BF16)
        xo_ref[...] = x1 + g2 * _swiglu(hb, wg_ref, wu_ref, wd_ref)


def _gmlp_in(x, sh1, sc1, gmix, win_ref, bin, lng, lnb):
    a = jax.nn.gelu(_dot(_modulated_norm(x, gmix, sh1, sc1), win_ref[...]) + bin)
    return a[:, :GMLP_WIDTH], _layernorm(a[:, GMLP_WIDTH:], lng, lnb)


def _l1_prompt_kernel(x_ref, mod_ref, gmix_ref, gffn_ref, gfin_ref, win_ref, bin_ref, lng_ref, lnb_ref,
                      ws_ref, bsf_ref, wout_ref, bout_ref, wg_ref, wu_ref, wd_ref,
                      yo_ref, um_scr, x1_scr, h2_scr, *, n_tiles, tiles_per_seq):
    c, gd = GMLP_CHUNK, GMLP_GROUP_DIM
    tt = x_ref.shape[0]
    s, _, seq_a, seq_b = _prompt_steps(n_tiles, tiles_per_seq)

    @pl.when(s == 0)
    def _():
        x1_scr[...] = jnp.zeros(x1_scr.shape, F32)
        h2_scr[...] = jnp.zeros(h2_scr.shape, BF16)

    x = x_ref[...]
    sh1, sc1, g1, sh2, sc2, _ = _split_mod(mod_ref, seq_a)
    u, v = _gmlp_in(x, sh1, sc1, gmix_ref[...], win_ref, bin_ref[...], lng_ref[...], lnb_ref[...])
    vb = v.astype(BF16)

    g2_b = _split_mod(mod_ref, seq_b)[5]
    x2 = x1_scr[...] + g2_b * _swiglu(h2_scr[...], wg_ref, wu_ref, wd_ref)
    yo_ref[...] = _rmsnorm(x2, gfin_ref[...])

    row = jax.lax.broadcasted_iota(jnp.int32, (c, c), 0)
    col = jax.lax.broadcasted_iota(jnp.int32, (c, c), 1)
    causal = row >= col
    for g in range(GMLP_GROUPS):
        cols = slice(g * gd, (g + 1) * gd)
        wt = jnp.where(causal, ws_ref[g], 0.0).astype(BF16)
        vg = jnp.concatenate([vb[r0:r0 + c, cols] for r0 in range(0, tt, c)], axis=1)
        m = jnp.dot(wt, vg, preferred_element_type=F32)
        for i, r0 in enumerate(range(0, tt, c)):
            mi = m[:, i * gd:(i + 1) * gd] + bsf_ref[:, cols]
            um_scr[r0:r0 + c, cols] = (u[r0:r0 + c, cols] * mi).astype(BF16)

    x1 = x + g1 * (jnp.dot(um_scr[...], wout_ref[...], preferred_element_type=F32) + bout_ref[...])
    x1_scr[...] = x1
    h2_scr[...] = _modulated_norm(x1, gffn_ref[...], sh2, sc2).astype(BF16)


def _l1_sample_kernel(x_ref, mod_ref, gmix_ref, gffn_ref, gfin_ref, win_ref, bin_ref, lng_ref, lnb_ref,
                      w00_ref, b0_ref, wout_ref, bout_ref, wg_ref, wu_ref, wd_ref, yo_ref, vo_ref):
    x = x_ref[...]
    sh1, sc1, g1, sh2, sc2, g2 = _split_mod(mod_ref)
    u, v = _gmlp_in(x, sh1, sc1, gmix_ref[...], win_ref, bin_ref[...], lng_ref[...], lnb_ref[...])
    vo_ref[...] = v
    um = u * (w00_ref[...] * v + b0_ref[...])
    x1 = x + g1 * (_dot(um, wout_ref[...]) + bout_ref[...])
    hb = _modulated_norm(x1, gffn_ref[...], sh2, sc2).astype(BF16)
    x2 = x1 + g2 * _swiglu(hb, wg_ref, wu_ref, wd_ref)
    yo_ref[...] = _rmsnorm(x2, gfin_ref[...])


def _resident(shape):
    zeros = (0,) * len(shape)
    return pl.BlockSpec(shape, lambda *_: zeros, pipeline_mode=pl.Buffered(1))


def _row(v):
    return v.reshape(1, -1)


def _params(n_grid):
    return pltpu.CompilerParams(dimension_semantics=("arbitrary",) * n_grid,
                                vmem_limit_bytes=VMEM_LIMIT_BYTES)


def _prompt_tiling(x):
    b, t, d = x.shape
    tt = PROMPT_TILE
    assert t % tt == 0 and tt % GMLP_CHUNK == 0 and tt >= CARRY_ROWS
    n_tiles = b * t // tt
    mixer_tile = pl.BlockSpec((tt, d), lambda s: (jnp.minimum(s, n_tiles - 1), 0))
    ffn_tile = pl.BlockSpec((tt, d), lambda s: (jnp.maximum(s - 1, 0), 0))
    return x.reshape(b * t, d), mixer_tile, ffn_tile, dict(n_tiles=n_tiles, tiles_per_seq=t // tt)


def _l0_prompt_call(x, mod, consts):
    b, t, d = x.shape
    tt = PROMPT_TILE
    x2d, mixer_tile, ffn_tile, counts = _prompt_tiling(x)
    n_tiles, tiles_per_seq = counts["n_tiles"], counts["tiles_per_seq"]
    state = pl.BlockSpec((None, None, CONV_CTX, d),
                         lambda s: (0, jnp.minimum(s, n_tiles - 1) // tiles_per_seq, 0, 0))
    xo, cs = pl.pallas_call(
        functools.partial(_l0_prompt_kernel, **counts),
        grid=(n_tiles + 1,),
        in_specs=[mixer_tile, _resident(mod.shape)] + [_resident(c.shape) for c in consts],
        out_specs=[ffn_tile, state],
        out_shape=[jax.ShapeDtypeStruct((b * t, d), F32), jax.ShapeDtypeStruct((1, b, CONV_CTX, d), F32)],
        scratch_shapes=[pltpu.VMEM((CARRY_ROWS + tt, d), F32),
                        pltpu.VMEM((SUBLANES - 1, CARRY_ROWS + tt - SUBLANES, LANES), F32),
                        pltpu.VMEM((tt, d), F32),
                        pltpu.VMEM((tt, d), F32),
                        pltpu.VMEM((tt, d), BF16)],
        compiler_params=_params(1),
        name="l0_prompt",
    )(x2d, mod, *consts)
    return xo.reshape(b, t, d), cs


def _l0_sample_call(x, mod, ctx, consts):
    n, d = x.shape
    sb = SAMPLE_BLOCK
    state = pl.BlockSpec((None, sb, CONV_CTX, d), lambda i: (0, i, 0, 0))
    return pl.pallas_call(
        _l0_sample_kernel,
        grid=(n // sb,),
        in_specs=[_resident(x.shape), _resident(mod.shape), state] + [_resident(c.shape) for c in consts],
        out_specs=[pl.BlockSpec((n, d), lambda i: (0, 0)), state],
        out_shape=[jax.ShapeDtypeStruct((n, d), F32), jax.ShapeDtypeStruct(ctx.shape, F32)],
        scratch_shapes=[pltpu.VMEM((n, d), F32), pltpu.VMEM((n, d), F32)],
        compiler_params=_params(1),
        name="l0_sample",
    )(x, mod, ctx, *consts)


def _l1_prompt_call(x, mod, consts):
    b, t, d = x.shape
    tt = PROMPT_TILE
    x2d, mixer_tile, ffn_tile, counts = _prompt_tiling(x)
    y = pl.pallas_call(
        functools.partial(_l1_prompt_kernel, **counts),
        grid=(counts["n_tiles"] + 1,),
        in_specs=[mixer_tile, _resident(mod.shape)] + [_resident(c.shape) for c in consts],
        out_specs=ffn_tile,
        out_shape=jax.ShapeDtypeStruct((b * t, d), F32),
        scratch_shapes=[pltpu.VMEM((tt, GMLP_WIDTH), BF16),
                        pltpu.VMEM((tt, d), F32),
                        pltpu.VMEM((tt, d), BF16)],
        compiler_params=_params(1),
        name="l1_prompt",
    )(x2d, mod, *consts)
    return y.reshape(b, t, d)


def _l1_sample_call(x, mod, consts):
    n, d = x.shape
    return pl.pallas_call(
        _l1_sample_kernel,
        grid=(1,),
        in_specs=[_resident(x.shape), _resident(mod.shape)] + [_resident(c.shape) for c in consts],
        out_specs=[pl.BlockSpec((n, d), lambda i: (0, 0)), pl.BlockSpec((n, GMLP_WIDTH), lambda i: (0, 0))],
        out_shape=[jax.ShapeDtypeStruct((n, d), F32), jax.ShapeDtypeStruct((n, GMLP_WIDTH), F32)],
        compiler_params=_params(1),
        name="l1_sample",
    )(x, mod, *consts)


def kernel(x_prompt, x_sample, c_prompt, c_sample, state_conv, w_ada, b_ada, norm_mix_g, norm_ffn_g, final_norm_g, conv_w_pw1, conv_b_pw1, conv_w_dw, conv_b_dw, conv_ln_g, conv_ln_b, conv_w_pw2, conv_b_pw2, gmlp_w_in, gmlp_b_in, gmlp_ln_g, gmlp_ln_b, gmlp_w_s, gmlp_b_s, gmlp_w_out, gmlp_b_out, ffn_w_gate, ffn_w_up, ffn_w_down):
    d = x_prompt.shape[-1]
    ns = x_sample.shape[0]
    assert x_sample.shape[1] == 1 and state_conv.shape[0] == 1 and gmlp_w_in.shape[0] == 1

    mod_p, mod_s = _ada_call(c_prompt, c_sample, w_ada, b_ada)

    ffn = [(ffn_w_gate[i].astype(BF16), ffn_w_up[i].astype(BF16), ffn_w_down[i].astype(BF16)) for i in range(2)]

    l0 = (_row(norm_mix_g[0]), _row(norm_ffn_g[0]), conv_w_pw1[0].astype(BF16), _row(conv_b_pw1[0]),
          conv_w_dw[0], _row(conv_b_dw[0]), _row(conv_ln_g[0]), _row(conv_ln_b[0]),
          conv_w_pw2[0].astype(BF16), _row(conv_b_pw2[0])) + ffn[0]
    l1_head = (_row(norm_mix_g[1]), _row(norm_ffn_g[1]), _row(final_norm_g), gmlp_w_in[0].astype(BF16),
               _row(gmlp_b_in[0]), _row(gmlp_ln_g[0]), _row(gmlp_ln_b[0]))
    l1_tail = (gmlp_w_out[0].astype(BF16), _row(gmlp_b_out[0])) + ffn[1]

    bias_full = jnp.repeat(gmlp_b_s[0].T, GMLP_GROUP_DIM, axis=1)
    w00 = _row(jnp.repeat(gmlp_w_s[0][:, 0, 0], GMLP_GROUP_DIM))
    b0 = _row(jnp.repeat(gmlp_b_s[0][:, 0], GMLP_GROUP_DIM))

    xp, cs_p = _l0_prompt_call(x_prompt, mod_p[0], l0)
    y_prompt = _l1_prompt_call(xp, mod_p[1], l1_head + (gmlp_w_s[0], bias_full) + l1_tail)

    xs, cs_s = _l0_sample_call(x_sample.reshape(ns, d), mod_s[0], state_conv, l0)
    y_sample, v_s = _l1_sample_call(xs, mod_s[1], l1_head + (w00, b0) + l1_tail)

    return (y_prompt, y_sample.reshape(ns, 1, d), cs_p, cs_s, v_s.reshape(1, ns, 1, GMLP_WIDTH))
```

```python
import functools
from typing import NamedTuple

import jax
import jax.numpy as jnp
from jax.experimental import pallas as pl
from jax.experimental.pallas import tpu as pltpu

D_MODEL = 1024
CONV_WIDTH = 31
CONV_CTX = CONV_WIDTH - 1
GMLP_CHUNK = 128
GMLP_WIDTH = 2 * D_MODEL
GMLP_GROUPS = 8
GMLP_GROUP_DIM = GMLP_WIDTH // GMLP_GROUPS
EPS = 1e-6

SUBLANES = 8
LANES = 128
CARRY_ROWS = 32
CARRY_SKEW = CARRY_ROWS - CONV_CTX
PROMPT_TILE = 512
CONV_ROW_BLOCK = 64
ADA_TILE_N = 1536
VMEM_LIMIT_BYTES = 60 * 1024 * 1024

F32 = jnp.float32
BF16 = jnp.bfloat16


def _dot(a, b):
    return jnp.dot(a.astype(BF16), b, preferred_element_type=F32)


def _rmsnorm(x, g):
    return x * jax.lax.rsqrt(jnp.mean(x * x, axis=-1, keepdims=True) + EPS) * g


def _layernorm(x, g, b):
    mu = jnp.mean(x, axis=-1, keepdims=True)
    xc = x - mu
    var = jnp.mean(xc * xc, axis=-1, keepdims=True)
    return xc * jax.lax.rsqrt(var + EPS) * g + b


def _silu(x):
    return x * jax.nn.sigmoid(x)


def _modulated_norm(x, g, shift, scale):
    return _rmsnorm(x, g) * (1.0 + scale) + shift


def _swiglu(hb, wg_ref, wu_ref, wd_ref):
    gate = jnp.dot(hb, wg_ref[...], preferred_element_type=F32)
    up = jnp.dot(hb, wu_ref[...], preferred_element_type=F32)
    return _dot(_silu(gate) * up, wd_ref[...])


def _split_mod(mod_ref, row=None):
    d = D_MODEL
    rows = slice(None) if row is None else pl.ds(row, 1)
    return [mod_ref[rows, i * d:(i + 1) * d] for i in range(6)]


def _ada_kernel(cp_ref, cs_ref, w_ref, b_ref, op_ref, os_ref):
    w = w_ref[...].astype(BF16)
    op_ref[...] = _dot(_silu(cp_ref[...]), w) + b_ref[...]
    os_ref[...] = _dot(_silu(cs_ref[...]), w) + b_ref[...]


def _ada_call(c_prompt, c_sample, w_ada, b_ada):
    depth, d, n = w_ada.shape
    col_tile = lambda rows: pl.BlockSpec((None, rows, ADA_TILE_N), lambda i, j: (i, 0, j))
    return pl.pallas_call(
        _ada_kernel,
        grid=(depth, n // ADA_TILE_N),
        in_specs=[
            pl.BlockSpec(c_prompt.shape, lambda i, j: (0, 0)),
            pl.BlockSpec(c_sample.shape, lambda i, j: (0, 0)),
            col_tile(d),
            col_tile(1),
        ],
        out_specs=[col_tile(c_prompt.shape[0]), col_tile(c_sample.shape[0])],
        out_shape=[jax.ShapeDtypeStruct((depth, c_prompt.shape[0], n), F32),
                   jax.ShapeDtypeStruct((depth, c_sample.shape[0], n), F32)],
        compiler_params=pltpu.CompilerParams(dimension_semantics=("arbitrary", "arbitrary")),
        name="adaln_mod",
    )(c_prompt, c_sample, w_ada, b_ada.reshape(depth, 1, n))


def _glu_in(x, sh1, sc1, gmix, wpw1_ref, bpw1):
    d = D_MODEL
    a = _dot(_modulated_norm(x, gmix, sh1, sc1), wpw1_ref[...]) + bpw1
    return a[:, :d] * jax.nn.sigmoid(a[:, d:])


def _conv_out(y, x, g1, lng, lnb, wpw2_ref, bpw2):
    y = _silu(_layernorm(y, lng, lnb))
    return x + g1 * (_dot(y, wpw2_ref[...]) + bpw2)


def _prompt_steps(n_tiles, tiles_per_seq):
    s = pl.program_id(0)
    ta = jnp.minimum(s, n_tiles - 1)
    tb = jnp.maximum(s - 1, 0)
    return s, ta, ta // tiles_per_seq, tb // tiles_per_seq


def _l0_prompt_kernel(x_ref, mod_ref, gmix_ref, gffn_ref, wpw1_ref, bpw1_ref, wdw_ref, bdw_ref,
                      lng_ref, lnb_ref, wpw2_ref, bpw2_ref, wg_ref, wu_ref, wd_ref,
                      xo_ref, cs_ref, zbuf, zsh, ybuf, x1_scr, h2_scr, *, n_tiles, tiles_per_seq):
    d = D_MODEL
    tt = x_ref.shape[0]
    s, ta, seq_a, seq_b = _prompt_steps(n_tiles, tiles_per_seq)

    @pl.when(s == 0)
    def _():
        x1_scr[...] = jnp.zeros(x1_scr.shape, F32)
        h2_scr[...] = jnp.zeros(h2_scr.shape, BF16)

    @pl.when(ta % tiles_per_seq == 0)
    def _():
        zbuf[0:CARRY_ROWS, :] = jnp.zeros((CARRY_ROWS, d), F32)

    x = x_ref[...]
    sh1, sc1, g1, sh2, sc2, _ = _split_mod(mod_ref, seq_a)
    zbuf[CARRY_ROWS:CARRY_ROWS + tt, :] = _glu_in(x, sh1, sc1, gmix_ref[...], wpw1_ref, bpw1_ref[...])

    g2_b = _split_mod(mod_ref, seq_b)[5]
    xo_ref[...] = x1_scr[...] + g2_b * _swiglu(h2_scr[...], wg_ref, wu_ref, wd_ref)

    n_sh = tt + CARRY_ROWS - SUBLANES
    rb = CONV_ROW_BLOCK
    for l0 in range(0, d, LANES):
        lanes = slice(l0, l0 + LANES)
        for sft in range(1, SUBLANES):
            zsh[sft - 1, :, :] = zbuf[sft:sft + n_sh, lanes]
        for r0 in range(0, tt, rb):
            acc = jnp.zeros((rb, LANES), F32)
            for k in range(CONV_WIDTH):
                q, sft = divmod(CARRY_SKEW + k, SUBLANES)
                rows = slice(r0 + q * SUBLANES, r0 + q * SUBLANES + rb)
                win = zbuf[rows, lanes] if sft == 0 else zsh[sft - 1, rows, :]
                acc = acc + wdw_ref[k:k + 1, lanes] * win
            ybuf[r0:r0 + rb, lanes] = acc + bdw_ref[:, lanes]

    x1 = _conv_out(ybuf[...], x, g1, lng_ref[...], lnb_ref[...], wpw2_ref, bpw2_ref[...])
    x1_scr[...] = x1
    h2_scr[...] = _modulated_norm(x1, gffn_ref[...], sh2, sc2).astype(BF16)

    @pl.when(ta % tiles_per_seq == tiles_per_seq - 1)
    def _():
        cs_ref[...] = zbuf[tt + CARRY_SKEW:tt + CARRY_ROWS, :]

    zbuf[0:CARRY_ROWS, :] = zbuf[tt:tt + CARRY_ROWS, :]


def _l0_sample_kernel(x_ref, mod_ref, ctx_ref, gmix_ref, gffn_ref, wpw1_ref, bpw1_ref, wdw_ref, bdw_ref,
                      lng_ref, lnb_ref, wpw2_ref, bpw2_ref, wg_ref, wu_ref, wd_ref,
                      xo_ref, cso_ref, z_scr, y_scr):
    k = pl.program_id(0)

    @pl.when(k == 0)
    def _():
        sh1, sc1 = _split_mod(mod_ref)[:2]
        z_scr[...] = _glu_in(x_ref[...], sh1, sc1, gmix_ref[...], wpw1_ref, bpw1_ref[...])
        y_scr[...] = jnp.zeros(y_scr.shape, F32) + bdw_ref[...]

    @pl.when(k < CONV_CTX)
    def _():
        y_scr[...] += wdw_ref[pl.ds(k, 1), :] * ctx_ref[...]
        cso_ref[...] = ctx_ref[...]

    @pl.when(k == CONV_CTX)
    def _():
        z = z_scr[...]
        cso_ref[...] = z
        y = y_scr[...] + wdw_ref[CONV_CTX:CONV_WIDTH, :] * z
        _, _, g1, sh2, sc2, g2 = _split_mod(mod_ref)
        x1 = _conv_out(y, x_ref[...], g1, lng_ref[...], lnb_ref[...], wpw2_ref, bpw2_ref[...])
        hb = _modulated_norm(x1, gffn_ref[...], sh2, sc2).astype(BF16)
        xo_ref[...] = x1 + g2 * _swiglu(hb, wg_ref, wu_ref, wd_ref)


def _gmlp_in(x, sh1, sc1, gmix, win_ref, bin, lng, lnb):
    a = jax.nn.gelu(_dot(_modulated_norm(x, gmix, sh1, sc1), win_ref[...]) + bin)
    return a[:, :GMLP_WIDTH], _layernorm(a[:, GMLP_WIDTH:], lng, lnb)


def _l1_prompt_kernel(x_ref, mod_ref, gmix_ref, gffn_ref, gfin_ref, win_ref, bin_ref, lng_ref, lnb_ref,
                      ws_ref, bsf_ref, wout_ref, bout_ref, wg_ref, wu_ref, wd_ref,
                      yo_ref, um_scr, x1_scr, h2_scr, *, n_tiles, tiles_per_seq):
    c, gd = GMLP_CHUNK, GMLP_GROUP_DIM
    tt = x_ref.shape[0]
    s, _, seq_a, seq_b = _prompt_steps(n_tiles, tiles_per_seq)

    @pl.when(s == 0)
    def _():
        x1_scr[...] = jnp.zeros(x1_scr.shape, F32)
        h2_scr[...] = jnp.zeros(h2_scr.shape, BF16)

    x = x_ref[...]
    sh1, sc1, g1, sh2, sc2, _ = _split_mod(mod_ref, seq_a)
    u, v = _gmlp_in(x, sh1, sc1, gmix_ref[...], win_ref, bin_ref[...], lng_ref[...], lnb_ref[...])
    vb = v.astype(BF16)

    g2_b = _split_mod(mod_ref, seq_b)[5]
    x2 = x1_scr[...] + g2_b * _swiglu(h2_scr[...], wg_ref, wu_ref, wd_ref)
    yo_ref[...] = _rmsnorm(x2, gfin_ref[...])

    row = jax.lax.broadcasted_iota(jnp.int32, (c, c), 0)
    col = jax.lax.broadcasted_iota(jnp.int32, (c, c), 1)
    causal = row >= col
    for g in range(GMLP_GROUPS):
        cols = slice(g * gd, (g + 1) * gd)
        wt = jnp.where(causal, ws_ref[g], 0.0).astype(BF16)
        vg = jnp.concatenate([vb[r0:r0 + c, cols] for r0 in range(0, tt, c)], axis=1)
        m = jnp.dot(wt, vg, preferred_element_type=F32)
        for i, r0 in enumerate(range(0, tt, c)):
            mi = m[:, i * gd:(i + 1) * gd] + bsf_ref[:, cols]
            um_scr[r0:r0 + c, cols] = (u[r0:r0 + c, cols] * mi).astype(BF16)

    x1 = x + g1 * (jnp.dot(um_scr[...], wout_ref[...], preferred_element_type=F32) + bout_ref[...])
    x1_scr[...] = x1
    h2_scr[...] = _modulated_norm(x1, gffn_ref[...], sh2, sc2).astype(BF16)


def _l1_sample_kernel(x_ref, mod_ref, gmix_ref, gffn_ref, gfin_ref, win_ref, bin_ref, lng_ref, lnb_ref,
                      w00_ref, b0_ref, wout_ref, bout_ref, wg_ref, wu_ref, wd_ref, yo_ref, vo_ref):
    x = x_ref[...]
    sh1, sc1, g1, sh2, sc2, g2 = _split_mod(mod_ref)
    u, v = _gmlp_in(x, sh1, sc1, gmix_ref[...], win_ref, bin_ref[...], lng_ref[...], lnb_ref[...])
    vo_ref[...] = v
    um = u * (w00_ref[...] * v + b0_ref[...])
    x1 = x + g1 * (_dot(um, wout_ref[...]) + bout_ref[...])
    hb = _modulated_norm(x1, gffn_ref[...], sh2, sc2).astype(BF16)
    x2 = x1 + g2 * _swiglu(hb, wg_ref, wu_ref, wd_ref)
    yo_ref[...] = _rmsnorm(x2, gfin_ref[...])


class _Layer(NamedTuple):
    stacked: jax.Array
    layer: int


def _resident(c):
    if isinstance(c, _Layer):
        index = (c.layer,) + (0,) * (c.stacked.ndim - 1)
        return pl.BlockSpec((None,) + c.stacked.shape[1:], lambda *_: index, pipeline_mode=pl.Buffered(1))
    index = (0,) * c.ndim
    return pl.BlockSpec(c.shape, lambda *_: index, pipeline_mode=pl.Buffered(1))


def _operand(c):
    return c.stacked if isinstance(c, _Layer) else c


def _row(v):
    return v.reshape(1, -1)


def _params(n_grid):
    return pltpu.CompilerParams(dimension_semantics=("arbitrary",) * n_grid,
                                vmem_limit_bytes=VMEM_LIMIT_BYTES)


def _prompt_tiling(x):
    b, t, d = x.shape
    tt = PROMPT_TILE
    assert t % tt == 0 and tt % GMLP_CHUNK == 0 and tt >= CARRY_ROWS
    n_tiles = b * t // tt
    mixer_tile = pl.BlockSpec((tt, d), lambda s: (jnp.minimum(s, n_tiles - 1), 0))
    ffn_tile = pl.BlockSpec((tt, d), lambda s: (jnp.maximum(s - 1, 0), 0))
    return x.reshape(b * t, d), mixer_tile, ffn_tile, dict(n_tiles=n_tiles, tiles_per_seq=t // tt)


def _l0_prompt_call(x, mod, consts):
    b, t, d = x.shape
    tt = PROMPT_TILE
    x2d, mixer_tile, ffn_tile, counts = _prompt_tiling(x)
    n_tiles, tiles_per_seq = counts["n_tiles"], counts["tiles_per_seq"]
    state = pl.BlockSpec((None, None, CONV_CTX, d),
                         lambda s: (0, jnp.minimum(s, n_tiles - 1) // tiles_per_seq, 0, 0))
    xo, cs = pl.pallas_call(
        functools.partial(_l0_prompt_kernel, **counts),
        grid=(n_tiles + 1,),
        in_specs=[mixer_tile, _resident(mod)] + [_resident(c) for c in consts],
        out_specs=[ffn_tile, state],
        out_shape=[jax.ShapeDtypeStruct((b * t, d), F32), jax.ShapeDtypeStruct((1, b, CONV_CTX, d), F32)],
        scratch_shapes=[pltpu.VMEM((CARRY_ROWS + tt, d), F32),
                        pltpu.VMEM((SUBLANES - 1, CARRY_ROWS + tt - SUBLANES, LANES), F32),
                        pltpu.VMEM((tt, d), F32),
                        pltpu.VMEM((tt, d), F32),
                        pltpu.VMEM((tt, d), BF16)],
        compiler_params=_params(1),
        name="l0_prompt",
    )(x2d, _operand(mod), *map(_operand, consts))
    return xo.reshape(b, t, d), cs


def _l0_sample_call(x, mod, ctx_rows, consts):
    n, d = x.shape
    row_in = pl.BlockSpec((None, None, n, d), lambda k: (0, jnp.minimum(k, CONV_CTX - 1), 0, 0))
    row_out = pl.BlockSpec((None, None, n, d), lambda k: (0, jnp.maximum(k - 1, 0), 0, 0))
    return pl.pallas_call(
        _l0_sample_kernel,
        grid=(CONV_WIDTH,),
        in_specs=[_resident(x), _resident(mod), row_in] + [_resident(c) for c in consts],
        out_specs=[pl.BlockSpec((n, d), lambda k: (0, 0)), row_out],
        out_shape=[jax.ShapeDtypeStruct((n, d), F32), jax.ShapeDtypeStruct(ctx_rows.shape, F32)],
        scratch_shapes=[pltpu.VMEM((n, d), F32), pltpu.VMEM((n, d), F32)],
        compiler_params=_params(1),
        name="l0_sample",
    )(x, _operand(mod), ctx_rows, *map(_operand, consts))


def _l1_prompt_call(x, mod, consts):
    b, t, d = x.shape
    tt = PROMPT_TILE
    x2d, mixer_tile, ffn_tile, counts = _prompt_tiling(x)
    y = pl.pallas_call(
        functools.partial(_l1_prompt_kernel, **counts),
        grid=(counts["n_tiles"] + 1,),
        in_specs=[mixer_tile, _resident(mod)] + [_resident(c) for c in consts],
        out_specs=ffn_tile,
        out_shape=jax.ShapeDtypeStruct((b * t, d), F32),
        scratch_shapes=[pltpu.VMEM((tt, GMLP_WIDTH), BF16),
                        pltpu.VMEM((tt, d), F32),
                        pltpu.VMEM((tt, d), BF16)],
        compiler_params=_params(1),
        name="l1_prompt",
    )(x2d, _operand(mod), *map(_operand, consts))
    return y.reshape(b, t, d)


def _l1_sample_call(x, mod, consts):
    n, d = x.shape
    return pl.pallas_call(
        _l1_sample_kernel,
        grid=(1,),
        in_specs=[_resident(x), _resident(mod)] + [_resident(c) for c in consts],
        out_specs=[pl.BlockSpec((n, d), lambda i: (0, 0)), pl.BlockSpec((n, GMLP_WIDTH), lambda i: (0, 0))],
        out_shape=[jax.ShapeDtypeStruct((n, d), F32), jax.ShapeDtypeStruct((n, GMLP_WIDTH), F32)],
        compiler_params=_params(1),
        name="l1_sample",
    )(x, _operand(mod), *map(_operand, consts))


def kernel(x_prompt, x_sample, c_prompt, c_sample, state_conv, w_ada, b_ada, norm_mix_g, norm_ffn_g, final_norm_g, conv_w_pw1, conv_b_pw1, conv_w_dw, conv_b_dw, conv_ln_g, conv_ln_b, conv_w_pw2, conv_b_pw2, gmlp_w_in, gmlp_b_in, gmlp_ln_g, gmlp_ln_b, gmlp_w_s, gmlp_b_s, gmlp_w_out, gmlp_b_out, ffn_w_gate, ffn_w_up, ffn_w_down):
    d = x_prompt.shape[-1]
    ns = x_sample.shape[0]
    assert x_sample.shape[1] == 1 and state_conv.shape[0] == 1 and gmlp_w_in.shape[0] == 1

    mod_p, mod_s = _ada_call(c_prompt, c_sample, w_ada, b_ada)

    ffn_bf16 = (ffn_w_gate.astype(BF16), ffn_w_up.astype(BF16), ffn_w_down.astype(BF16))
    ffn = [tuple(_Layer(w, i) for w in ffn_bf16) for i in range(2)]

    l0 = (_row(norm_mix_g[0]), _row(norm_ffn_g[0]), _Layer(conv_w_pw1.astype(BF16), 0), _row(conv_b_pw1[0]),
          _Layer(conv_w_dw, 0), _row(conv_b_dw[0]), _row(conv_ln_g[0]), _row(conv_ln_b[0]),
          _Layer(conv_w_pw2.astype(BF16), 0), _row(conv_b_pw2[0])) + ffn[0]
    l1_head = (_row(norm_mix_g[1]), _row(norm_ffn_g[1]), _row(final_norm_g), _Layer(gmlp_w_in.astype(BF16), 0),
               _row(gmlp_b_in[0]), _row(gmlp_ln_g[0]), _row(gmlp_ln_b[0]))
    l1_tail = (_Layer(gmlp_w_out.astype(BF16), 0), _row(gmlp_b_out[0])) + ffn[1]

    bias_full = jnp.repeat(gmlp_b_s[0].T, GMLP_GROUP_DIM, axis=1)
    w00 = _row(jnp.repeat(gmlp_w_s[0][:, 0, 0], GMLP_GROUP_DIM))
    b0 = _row(jnp.repeat(gmlp_b_s[0][:, 0], GMLP_GROUP_DIM))

    xp, cs_p = _l0_prompt_call(x_prompt, _Layer(mod_p, 0), l0)
    y_prompt = _l1_prompt_call(xp, _Layer(mod_p, 1), l1_head + (_Layer(gmlp_w_s, 0), bias_full) + l1_tail)

    xs, cs_rows = _l0_sample_call(x_sample.reshape(ns, d), _Layer(mod_s, 0), jnp.swapaxes(state_conv, 1, 2), l0)
    y_sample, v_s = _l1_sample_call(xs, _Layer(mod_s, 1), l1_head + (w00, b0) + l1_tail)

    return (y_prompt, y_sample.reshape(ns, 1, d), cs_p, jnp.swapaxes(cs_rows, 1, 2),
            v_s.reshape(1, ns, 1, GMLP_WIDTH))
```

```python
import functools
from typing import NamedTuple

import jax
import jax.numpy as jnp
from jax.experimental import pallas as pl
from jax.experimental.pallas import tpu as pltpu

D_MODEL = 1024
CONV_WIDTH = 31
CONV_CTX = CONV_WIDTH - 1
GMLP_CHUNK = 128
GMLP_WIDTH = 2 * D_MODEL
GMLP_GROUPS = 8
GMLP_GROUP_DIM = GMLP_WIDTH // GMLP_GROUPS
EPS = 1e-6

SUBLANES = 8
BF16_SUBLANES = 16
LANES = 128
CARRY_ROWS = 32
CARRY_SKEW = CARRY_ROWS - CONV_CTX
L0_PROMPT_TILE = 256
L1_PROMPT_TILE = 512
CONV_ROW_BLOCK = 64
ADA_TILE_N = 1536
VMEM_LIMIT_BYTES = 60 * 1024 * 1024

F32 = jnp.float32
BF16 = jnp.bfloat16


def _dot(a, b):
    return jnp.dot(a.astype(BF16), b, preferred_element_type=F32)


def _rmsnorm(x, g):
    return x * jax.lax.rsqrt(jnp.mean(x * x, axis=-1, keepdims=True) + EPS) * g


def _layernorm(x, g, b):
    mu = jnp.mean(x, axis=-1, keepdims=True)
    xc = x - mu
    var = jnp.mean(xc * xc, axis=-1, keepdims=True)
    return xc * jax.lax.rsqrt(var + EPS) * g + b


def _silu(x):
    return x * jax.nn.sigmoid(x)


def _modulated_norm(x, g, shift, scale):
    return _rmsnorm(x, g) * (1.0 + scale) + shift


def _swiglu(hb, wg_ref, wu_ref, wd_ref):
    gate = jnp.dot(hb, wg_ref[...], preferred_element_type=F32)
    up = jnp.dot(hb, wu_ref[...], preferred_element_type=F32)
    return _dot(_silu(gate) * up, wd_ref[...])


def _split_mod(mod_ref, row=None):
    d = D_MODEL
    rows = slice(None) if row is None else pl.ds(row, 1)
    return [mod_ref[rows, i * d:(i + 1) * d] for i in range(6)]


def _ada_kernel(cp_ref, cs_ref, w_ref, b_ref, op_ref, os_ref):
    w = w_ref[...].astype(BF16)
    op_ref[...] = _dot(_silu(cp_ref[...]), w) + b_ref[...]
    os_ref[...] = _dot(_silu(cs_ref[...]), w) + b_ref[...]


def _ada_call(c_prompt, c_sample, w_ada, b_ada):
    depth, d, n = w_ada.shape
    col_tile = lambda rows: pl.BlockSpec((None, rows, ADA_TILE_N), lambda i, j: (i, 0, j))
    return pl.pallas_call(
        _ada_kernel,
        grid=(depth, n // ADA_TILE_N),
        in_specs=[
            pl.BlockSpec(c_prompt.shape, lambda i, j: (0, 0)),
            pl.BlockSpec(c_sample.shape, lambda i, j: (0, 0)),
            col_tile(d),
            col_tile(1),
        ],
        out_specs=[col_tile(c_prompt.shape[0]), col_tile(c_sample.shape[0])],
        out_shape=[jax.ShapeDtypeStruct((depth, c_prompt.shape[0], n), F32),
                   jax.ShapeDtypeStruct((depth, c_sample.shape[0], n), F32)],
        compiler_params=pltpu.CompilerParams(dimension_semantics=("arbitrary", "arbitrary")),
        name="adaln_mod",
    )(c_prompt, c_sample, w_ada, b_ada.reshape(depth, 1, n))


def _glu_in(x, sh1, sc1, gmix, wpw1_ref, bpw1):
    d = D_MODEL
    a = _dot(_modulated_norm(x, gmix, sh1, sc1), wpw1_ref[...]) + bpw1
    return a[:, :d] * jax.nn.sigmoid(a[:, d:])


def _conv_out(y, x, g1, lng, lnb, wpw2_ref, bpw2):
    y = _silu(_layernorm(y, lng, lnb))
    return x + g1 * (_dot(y, wpw2_ref[...]) + bpw2)


def _prompt_steps(n_tiles, tiles_per_seq):
    s = pl.program_id(0)
    ta = jnp.minimum(s, n_tiles - 1)
    tb = jnp.maximum(s - 1, 0)
    return s, ta, ta // tiles_per_seq, tb // tiles_per_seq


N_L0_CONSTS = 13


def _l0_prompt_kernel(x_ref, mod_ref, *refs, n_tiles, tiles_per_seq, n_cast):
    (gmix_ref, gffn_ref, wpw1_ref, bpw1_ref, wdw_ref, bdw_ref, lng_ref, lnb_ref, wpw2_ref, bpw2_ref,
     wg_ref, wu_ref, wd_ref) = refs[:N_L0_CONSTS]
    cast_src = refs[N_L0_CONSTS:N_L0_CONSTS + n_cast]
    xo_ref, cs_ref = refs[N_L0_CONSTS + n_cast:N_L0_CONSTS + n_cast + 2]
    cast_dst = refs[N_L0_CONSTS + n_cast + 2:N_L0_CONSTS + 2 * n_cast + 2]
    zbuf, zsh, ybuf, x1_scr, h2_scr = refs[N_L0_CONSTS + 2 * n_cast + 2:]
    d = D_MODEL
    tt = x_ref.shape[0]
    s, ta, seq_a, seq_b = _prompt_steps(n_tiles, tiles_per_seq)

    @pl.when(s == 0)
    def _():
        x1_scr[...] = jnp.zeros(x1_scr.shape, F32)
        h2_scr[...] = jnp.zeros(h2_scr.shape, BF16)

    @pl.when(ta % tiles_per_seq == 0)
    def _():
        zbuf[0:CARRY_ROWS, :] = jnp.zeros((CARRY_ROWS, d), F32)

    for src, dst in zip(cast_src, cast_dst):
        dst[...] = src[...].astype(BF16)

    x = x_ref[...]
    sh1, sc1, g1, sh2, sc2, _ = _split_mod(mod_ref, seq_a)
    zbuf[CARRY_ROWS:CARRY_ROWS + tt, :] = _glu_in(x, sh1, sc1, gmix_ref[...], wpw1_ref, bpw1_ref[...])

    g2_b = _split_mod(mod_ref, seq_b)[5]
    xo_ref[...] = x1_scr[...] + g2_b * _swiglu(h2_scr[...], wg_ref, wu_ref, wd_ref)

    n_sh = tt + CARRY_ROWS - SUBLANES
    rb = CONV_ROW_BLOCK
    for l0 in range(0, d, LANES):
        lanes = slice(l0, l0 + LANES)
        for sft in range(1, SUBLANES):
            zsh[sft - 1, :, :] = zbuf[sft:sft + n_sh, lanes]
        for r0 in range(0, tt, rb):
            acc = jnp.zeros((rb, LANES), F32)
            for k in range(CONV_WIDTH):
                q, sft = divmod(CARRY_SKEW + k, SUBLANES)
                rows = slice(r0 + q * SUBLANES, r0 + q * SUBLANES + rb)
                win = zbuf[rows, lanes] if sft == 0 else zsh[sft - 1, rows, :]
                acc = acc + wdw_ref[k:k + 1, lanes] * win
            ybuf[r0:r0 + rb, lanes] = acc + bdw_ref[:, lanes]

    x1 = _conv_out(ybuf[...], x, g1, lng_ref[...], lnb_ref[...], wpw2_ref, bpw2_ref[...])
    x1_scr[...] = x1
    h2_scr[...] = _modulated_norm(x1, gffn_ref[...], sh2, sc2).astype(BF16)

    @pl.when(ta % tiles_per_seq == tiles_per_seq - 1)
    def _():
        cs_ref[...] = zbuf[tt + CARRY_SKEW:tt + CARRY_ROWS, :]

    zbuf[0:CARRY_ROWS, :] = zbuf[tt:tt + CARRY_ROWS, :]


def _l0_sample_kernel(x_ref, mod_ref, ctx_ref, gmix_ref, gffn_ref, wpw1_ref, bpw1_ref, wdw_ref, bdw_ref,
                      lng_ref, lnb_ref, wpw2_ref, bpw2_ref, wg_ref, wu_ref, wd_ref,
                      xo_ref, cso_ref, z_scr, y_scr):
    k = pl.program_id(0)

    @pl.when(k == 0)
    def _():
        sh1, sc1 = _split_mod(mod_ref)[:2]
        z_scr[...] = _glu_in(x_ref[...], sh1, sc1, gmix_ref[...], wpw1_ref, bpw1_ref[...])
        y_scr[...] = jnp.zeros(y_scr.shape, F32) + bdw_ref[...]

    @pl.when(k < CONV_CTX)
    def _():
        y_scr[...] += wdw_ref[pl.ds(k, 1), :] * ctx_ref[...]
        cso_ref[...] = ctx_ref[...]

    @pl.when(k == CONV_CTX)
    def _():
        z = z_scr[...]
        cso_ref[...] = z
        y = y_scr[...] + wdw_ref[CONV_CTX:CONV_WIDTH, :] * z
        _, _, g1, sh2, sc2, g2 = _split_mod(mod_ref)
        x1 = _conv_out(y, x_ref[...], g1, lng_ref[...], lnb_ref[...], wpw2_ref, bpw2_ref[...])
        hb = _modulated_norm(x1, gffn_ref[...], sh2, sc2).astype(BF16)
        xo_ref[...] = x1 + g2 * _swiglu(hb, wg_ref, wu_ref, wd_ref)


def _gmlp_in(x, sh1, sc1, gmix, win_ref, bin, lng, lnb):
    a = jax.nn.gelu(_dot(_modulated_norm(x, gmix, sh1, sc1), win_ref[...]) + bin)
    return a[:, :GMLP_WIDTH], _layernorm(a[:, GMLP_WIDTH:], lng, lnb)


def _l1_prompt_kernel(x_ref, mod_ref, gmix_ref, gffn_ref, gfin_ref, win_ref, bin_ref, lng_ref, lnb_ref,
                      ws_ref, bsf_ref, wout_ref, bout_ref, wg_ref, wu_ref, wd_ref,
                      yo_ref, um_scr, x1_scr, h2_scr, *, n_tiles, tiles_per_seq):
    c, gd = GMLP_CHUNK, GMLP_GROUP_DIM
    tt = x_ref.shape[0]
    s, _, seq_a, seq_b = _prompt_steps(n_tiles, tiles_per_seq)

    @pl.when(s == 0)
    def _():
        x1_scr[...] = jnp.zeros(x1_scr.shape, F32)
        h2_scr[...] = jnp.zeros(h2_scr.shape, BF16)

    x = x_ref[...]
    sh1, sc1, g1, sh2, sc2, _ = _split_mod(mod_ref, seq_a)
    u, v = _gmlp_in(x, sh1, sc1, gmix_ref[...], win_ref, bin_ref[...], lng_ref[...], lnb_ref[...])
    vb = v.astype(BF16)

    g2_b = _split_mod(mod_ref, seq_b)[5]
    x2 = x1_scr[...] + g2_b * _swiglu(h2_scr[...], wg_ref, wu_ref, wd_ref)
    yo_ref[...] = _rmsnorm(x2, gfin_ref[...])

    row = jax.lax.broadcasted_iota(jnp.int32, (c, c), 0)
    col = jax.lax.broadcasted_iota(jnp.int32, (c, c), 1)
    causal = row >= col
    for g in range(GMLP_GROUPS):
        cols = slice(g * gd, (g + 1) * gd)
        wt = jnp.where(causal, ws_ref[g], 0.0).astype(BF16)
        vg = jnp.concatenate([vb[r0:r0 + c, cols] for r0 in range(0, tt, c)], axis=1)
        m = jnp.dot(wt, vg, preferred_element_type=F32)
        for i, r0 in enumerate(range(0, tt, c)):
            mi = m[:, i * gd:(i + 1) * gd] + bsf_ref[:, cols]
            um_scr[r0:r0 + c, cols] = (u[r0:r0 + c, cols] * mi).astype(BF16)

    x1 = x + g1 * (jnp.dot(um_scr[...], wout_ref[...], preferred_element_type=F32) + bout_ref[...])
    x1_scr[...] = x1
    h2_scr[...] = _modulated_norm(x1, gffn_ref[...], sh2, sc2).astype(BF16)


def _l1_sample_kernel(x_ref, mod_ref, gmix_ref, gffn_ref, gfin_ref, win_ref, bin_ref, lng_ref, lnb_ref,
                      w00_ref, b0_ref, wout_ref, bout_ref, wg_ref, wu_ref, wd_ref, yo_ref, vo_ref):
    x = x_ref[...]
    sh1, sc1, g1, sh2, sc2, g2 = _split_mod(mod_ref)
    u, v = _gmlp_in(x, sh1, sc1, gmix_ref[...], win_ref, bin_ref[...], lng_ref[...], lnb_ref[...])
    vo_ref[...] = v
    um = u * (w00_ref[...] * v + b0_ref[...])
    x1 = x + g1 * (_dot(um, wout_ref[...]) + bout_ref[...])
    hb = _modulated_norm(x1, gffn_ref[...], sh2, sc2).astype(BF16)
    x2 = x1 + g2 * _swiglu(hb, wg_ref, wu_ref, wd_ref)
    yo_ref[...] = _rmsnorm(x2, gfin_ref[...])


class _Layer(NamedTuple):
    stacked: jax.Array
    layer: int


def _resident(c):
    if isinstance(c, _Layer):
        index = (c.layer,) + (0,) * (c.stacked.ndim - 1)
        return pl.BlockSpec((None,) + c.stacked.shape[1:], lambda *_: index, pipeline_mode=pl.Buffered(1))
    index = (0,) * c.ndim
    return pl.BlockSpec(c.shape, lambda *_: index, pipeline_mode=pl.Buffered(1))


def _operand(c):
    return c.stacked if isinstance(c, _Layer) else c


def _row(v):
    return v.reshape(1, -1)


def _params(n_grid):
    return pltpu.CompilerParams(dimension_semantics=("arbitrary",) * n_grid,
                                vmem_limit_bytes=VMEM_LIMIT_BYTES)


def _prompt_tiling(x, tt):
    b, t, d = x.shape
    assert t % tt == 0 and tt % GMLP_CHUNK == 0 and tt >= CARRY_ROWS
    n_tiles = b * t // tt
    mixer_tile = pl.BlockSpec((tt, d), lambda s: (jnp.minimum(s, n_tiles - 1), 0))
    ffn_tile = pl.BlockSpec((tt, d), lambda s: (jnp.maximum(s - 1, 0), 0))
    return x.reshape(b * t, d), mixer_tile, ffn_tile, dict(n_tiles=n_tiles, tiles_per_seq=t // tt)


def _cast_blocks(w, n_steps):
    rows, cols = w.stacked.shape[1:]
    blk = next(r for r in range(BF16_SUBLANES, rows + 1, BF16_SUBLANES) if rows % r == 0 and rows // r <= n_steps)
    last = rows // blk - 1
    src = pl.BlockSpec((None, blk, cols), lambda s: (w.layer, jnp.minimum(s, last), 0))
    dst = pl.BlockSpec((blk, cols), lambda s: (jnp.minimum(s, last), 0))
    return src, dst, jax.ShapeDtypeStruct((rows, cols), BF16)


def _l0_prompt_call(x, mod, consts, cast):
    b, t, d = x.shape
    tt = L0_PROMPT_TILE
    x2d, mixer_tile, ffn_tile, counts = _prompt_tiling(x, tt)
    n_tiles, tiles_per_seq = counts["n_tiles"], counts["tiles_per_seq"]
    assert len(consts) == N_L0_CONSTS
    state = pl.BlockSpec((None, None, CONV_CTX, d),
                         lambda s: (0, jnp.minimum(s, n_tiles - 1) // tiles_per_seq, 0, 0))
    cast_src, cast_dst, cast_shapes = zip(*[_cast_blocks(w, n_tiles + 1) for w in cast])
    xo, cs, *cast_out = pl.pallas_call(
        functools.partial(_l0_prompt_kernel, n_cast=len(cast), **counts),
        grid=(n_tiles + 1,),
        in_specs=[mixer_tile, _resident(mod)] + [_resident(c) for c in consts] + list(cast_src),
        out_specs=[ffn_tile, state] + list(cast_dst),
        out_shape=[jax.ShapeDtypeStruct((b * t, d), F32), jax.ShapeDtypeStruct((1, b, CONV_CTX, d), F32)]
        + list(cast_shapes),
        scratch_shapes=[pltpu.VMEM((CARRY_ROWS + tt, d), F32),
                        pltpu.VMEM((SUBLANES - 1, CARRY_ROWS + tt - SUBLANES, LANES), F32),
                        pltpu.VMEM((tt, d), F32),
                        pltpu.VMEM((tt, d), F32),
                        pltpu.VMEM((tt, d), BF16)],
        compiler_params=_params(1),
        name="l0_prompt",
    )(x2d, _operand(mod), *map(_operand, consts), *map(_operand, cast))
    return xo.reshape(b, t, d), cs, cast_out


def _l0_sample_call(x, mod, ctx_rows, consts):
    n, d = x.shape
    row_in = pl.BlockSpec((None, None, n, d), lambda k: (0, jnp.minimum(k, CONV_CTX - 1), 0, 0))
    row_out = pl.BlockSpec((None, None, n, d), lambda k: (0, jnp.maximum(k - 1, 0), 0, 0))
    return pl.pallas_call(
        _l0_sample_kernel,
        grid=(CONV_WIDTH,),
        in_specs=[_resident(x), _resident(mod), row_in] + [_resident(c) for c in consts],
        out_specs=[pl.BlockSpec((n, d), lambda k: (0, 0)), row_out],
        out_shape=[jax.ShapeDtypeStruct((n, d), F32), jax.ShapeDtypeStruct(ctx_rows.shape, F32)],
        scratch_shapes=[pltpu.VMEM((n, d), F32), pltpu.VMEM((n, d), F32)],
        compiler_params=_params(1),
        name="l0_sample",
    )(x, _operand(mod), ctx_rows, *map(_operand, consts))


def _l1_prompt_call(x, mod, consts):
    b, t, d = x.shape
    tt = L1_PROMPT_TILE
    x2d, mixer_tile, ffn_tile, counts = _prompt_tiling(x, tt)
    y = pl.pallas_call(
        functools.partial(_l1_prompt_kernel, **counts),
        grid=(counts["n_tiles"] + 1,),
        in_specs=[mixer_tile, _resident(mod)] + [_resident(c) for c in consts],
        out_specs=ffn_tile,
        out_shape=jax.ShapeDtypeStruct((b * t, d), F32),
        scratch_shapes=[pltpu.VMEM((tt, GMLP_WIDTH), BF16),
                        pltpu.VMEM((tt, d), F32),
                        pltpu.VMEM((tt, d), BF16)],
        compiler_params=_params(1),
        name="l1_prompt",
    )(x2d, _operand(mod), *map(_operand, consts))
    return y.reshape(b, t, d)


def _l1_sample_call(x, mod, consts):
    n, d = x.shape
    return pl.pallas_call(
        _l1_sample_kernel,
        grid=(1,),
        in_specs=[_resident(x), _resident(mod)] + [_resident(c) for c in consts],
        out_specs=[pl.BlockSpec((n, d), lambda i: (0, 0)), pl.BlockSpec((n, GMLP_WIDTH), lambda i: (0, 0))],
        out_shape=[jax.ShapeDtypeStruct((n, d), F32), jax.ShapeDtypeStruct((n, GMLP_WIDTH), F32)],
        compiler_params=_params(1),
        name="l1_sample",
    )(x, _operand(mod), *map(_operand, consts))


def kernel(x_prompt, x_sample, c_prompt, c_sample, state_conv, w_ada, b_ada, norm_mix_g, norm_ffn_g, final_norm_g, conv_w_pw1, conv_b_pw1, conv_w_dw, conv_b_dw, conv_ln_g, conv_ln_b, conv_w_pw2, conv_b_pw2, gmlp_w_in, gmlp_b_in, gmlp_ln_g, gmlp_ln_b, gmlp_w_s, gmlp_b_s, gmlp_w_out, gmlp_b_out, ffn_w_gate, ffn_w_up, ffn_w_down):
    d = x_prompt.shape[-1]
    ns = x_sample.shape[0]
    assert x_sample.shape[1] == 1 and state_conv.shape[0] == 1 and gmlp_w_in.shape[0] == 1

    mod_p, mod_s = _ada_call(c_prompt, c_sample, w_ada, b_ada)

    l0 = (_row(norm_mix_g[0]), _row(norm_ffn_g[0]), _Layer(conv_w_pw1.astype(BF16), 0), _row(conv_b_pw1[0]),
          _Layer(conv_w_dw, 0), _row(conv_b_dw[0]), _row(conv_ln_g[0]), _row(conv_ln_b[0]),
          _Layer(conv_w_pw2.astype(BF16), 0), _row(conv_b_pw2[0]),
          ffn_w_gate[0].astype(BF16), ffn_w_up[0].astype(BF16), ffn_w_down[0].astype(BF16))

    l1_f32 = (_Layer(gmlp_w_in, 0), _Layer(gmlp_w_out, 0),
              _Layer(ffn_w_gate, 1), _Layer(ffn_w_up, 1), _Layer(ffn_w_down, 1))
    xp, cs_p, (w_in, w_out, w_gate, w_up, w_down) = _l0_prompt_call(x_prompt, _Layer(mod_p, 0), l0, l1_f32)

    l1_head = (_row(norm_mix_g[1]), _row(norm_ffn_g[1]), _row(final_norm_g), w_in,
               _row(gmlp_b_in[0]), _row(gmlp_ln_g[0]), _row(gmlp_ln_b[0]))
    l1_tail = (w_out, _row(gmlp_b_out[0]), w_gate, w_up, w_down)

    bias_full = jnp.repeat(gmlp_b_s[0].T, GMLP_GROUP_DIM, axis=1)
    w00 = _row(jnp.repeat(gmlp_w_s[0][:, 0, 0], GMLP_GROUP_DIM))
    b0 = _row(jnp.repeat(gmlp_b_s[0][:, 0], GMLP_GROUP_DIM))

    y_prompt = _l1_prompt_call(xp, _Layer(mod_p, 1), l1_head + (_Layer(gmlp_w_s, 0), bias_full) + l1_tail)

    xs, cs_rows = _l0_sample_call(x_sample.reshape(ns, d), _Layer(mod_s, 0), jnp.swapaxes(state_conv, 1, 2), l0)
    y_sample, v_s = _l1_sample_call(xs, _Layer(mod_s, 1), l1_head + (w00, b0) + l1_tail)

    return (y_prompt, y_sample.reshape(ns, 1, d), cs_p, jnp.swapaxes(cs_rows, 1, 2),
            v_s.reshape(1, ns, 1, GMLP_WIDTH))
```

```python
import functools
from typing import NamedTuple

import jax
import jax.numpy as jnp
from jax.experimental import pallas as pl
from jax.experimental.pallas import tpu as pltpu

D_MODEL = 1024
CONV_WIDTH = 31
CONV_CTX = CONV_WIDTH - 1
GMLP_CHUNK = 128
GMLP_WIDTH = 2 * D_MODEL
GMLP_GROUPS = 8
GMLP_GROUP_DIM = GMLP_WIDTH // GMLP_GROUPS
EPS = 1e-6

SUBLANES = 8
BF16_SUBLANES = 16
LANES = 128
CARRY_ROWS = 32
CARRY_SKEW = CARRY_ROWS - CONV_CTX
L0_PROMPT_TILE = 256
L1_PROMPT_TILE = 512
CONV_ROW_BLOCK = 64
ADA_TILE_N = 1536
VMEM_LIMIT_BYTES = 60 * 1024 * 1024

F32 = jnp.float32
BF16 = jnp.bfloat16


def _dot(a, b):
    return jnp.dot(a.astype(BF16), b, preferred_element_type=F32)


def _rmsnorm(x, g):
    return x * jax.lax.rsqrt(jnp.mean(x * x, axis=-1, keepdims=True) + EPS) * g


def _layernorm(x, g, b):
    mu = jnp.mean(x, axis=-1, keepdims=True)
    xc = x - mu
    var = jnp.mean(xc * xc, axis=-1, keepdims=True)
    return xc * jax.lax.rsqrt(var + EPS) * g + b


def _silu(x):
    return x * jax.nn.sigmoid(x)


def _modulated_norm(x, g, shift, scale):
    return _rmsnorm(x, g) * (1.0 + scale) + shift


def _swiglu(hb, wg_ref, wu_ref, wd_ref):
    gate = jnp.dot(hb, wg_ref[...], preferred_element_type=F32)
    up = jnp.dot(hb, wu_ref[...], preferred_element_type=F32)
    return _dot(_silu(gate) * up, wd_ref[...])


def _split_mod(mod_ref, row=None):
    d = D_MODEL
    rows = slice(None) if row is None else pl.ds(row, 1)
    return [mod_ref[rows, i * d:(i + 1) * d] for i in range(6)]


def _ada_kernel(cp_ref, cs_ref, w_ref, b_ref, op_ref, os_ref):
    w = w_ref[...].astype(BF16)
    op_ref[...] = _dot(_silu(cp_ref[...]), w) + b_ref[...]
    os_ref[...] = _dot(_silu(cs_ref[...]), w) + b_ref[...]


def _ada_call(c_prompt, c_sample, w_ada, b_ada):
    depth, d, n = w_ada.shape
    col_tile = lambda rows: pl.BlockSpec((None, rows, ADA_TILE_N), lambda i, j: (i, 0, j))
    return pl.pallas_call(
        _ada_kernel,
        grid=(depth, n // ADA_TILE_N),
        in_specs=[
            pl.BlockSpec(c_prompt.shape, lambda i, j: (0, 0)),
            pl.BlockSpec(c_sample.shape, lambda i, j: (0, 0)),
            col_tile(d),
            col_tile(1),
        ],
        out_specs=[col_tile(c_prompt.shape[0]), col_tile(c_sample.shape[0])],
        out_shape=[jax.ShapeDtypeStruct((depth, c_prompt.shape[0], n), F32),
                   jax.ShapeDtypeStruct((depth, c_sample.shape[0], n), F32)],
        compiler_params=pltpu.CompilerParams(dimension_semantics=("arbitrary", "arbitrary")),
        name="adaln_mod",
    )(c_prompt, c_sample, w_ada, b_ada.reshape(depth, 1, n))


def _glu_column(hb, wpw1_ref, bpw1_ref, j):
    cols = slice(2 * j * LANES, 2 * (j + 1) * LANES)
    a = jnp.dot(hb, wpw1_ref[:, cols], preferred_element_type=F32) + bpw1_ref[:, cols]
    return a[:, :LANES] * jax.nn.sigmoid(a[:, LANES:])


def _pair_glu_columns(w):
    lead = w.shape[:-1]
    w = w.reshape(lead + (2, D_MODEL // LANES, LANES))
    return jnp.swapaxes(w, -3, -2).reshape(lead + (2 * D_MODEL,))


def _conv_out(y, x, g1, lng, lnb, wpw2_ref, bpw2):
    y = _silu(_layernorm(y, lng, lnb))
    return x + g1 * (_dot(y, wpw2_ref[...]) + bpw2)


def _prompt_steps(n_tiles, tiles_per_seq):
    s = pl.program_id(0)
    ta = jnp.minimum(s, n_tiles - 1)
    tb = jnp.maximum(s - 1, 0)
    return s, ta, ta // tiles_per_seq, tb // tiles_per_seq


N_L0_CONSTS = 13


def _l0_prompt_kernel(x_ref, mod_ref, *refs, n_tiles, tiles_per_seq, n_cast):
    (gmix_ref, gffn_ref, wpw1_ref, bpw1_ref, wdw_ref, bdw_ref, lng_ref, lnb_ref, wpw2_ref, bpw2_ref,
     wg_ref, wu_ref, wd_ref) = refs[:N_L0_CONSTS]
    cast_src = refs[N_L0_CONSTS:N_L0_CONSTS + n_cast]
    xo_ref, cs_ref = refs[N_L0_CONSTS + n_cast:N_L0_CONSTS + n_cast + 2]
    cast_dst = refs[N_L0_CONSTS + n_cast + 2:N_L0_CONSTS + 2 * n_cast + 2]
    zbuf, zsh, ybuf, x1_scr, h2_scr = refs[N_L0_CONSTS + 2 * n_cast + 2:]
    d = D_MODEL
    tt = x_ref.shape[0]
    s, ta, seq_a, seq_b = _prompt_steps(n_tiles, tiles_per_seq)

    @pl.when(s == 0)
    def _():
        x1_scr[...] = jnp.zeros(x1_scr.shape, F32)
        h2_scr[...] = jnp.zeros(h2_scr.shape, BF16)

    @pl.when(ta % tiles_per_seq == 0)
    def _():
        zbuf[0:CARRY_ROWS, :] = jnp.zeros((CARRY_ROWS, d), F32)

    for src, dst in zip(cast_src, cast_dst):
        dst[...] = src[...].astype(BF16)

    x = x_ref[...]
    sh1, sc1, g1, sh2, sc2, _ = _split_mod(mod_ref, seq_a)
    hb = _modulated_norm(x, gmix_ref[...], sh1, sc1).astype(BF16)

    n_sh = tt + CARRY_ROWS - SUBLANES
    rb = CONV_ROW_BLOCK
    for j in range(d // LANES):
        lanes = slice(j * LANES, (j + 1) * LANES)
        zbuf[CARRY_ROWS:CARRY_ROWS + tt, lanes] = _glu_column(hb, wpw1_ref, bpw1_ref, j)
        for sft in range(1, SUBLANES):
            zsh[sft - 1, :, :] = zbuf[sft:sft + n_sh, lanes]
        for r0 in range(0, tt, rb):
            acc = jnp.zeros((rb, LANES), F32)
            for k in range(CONV_WIDTH):
                q, sft = divmod(CARRY_SKEW + k, SUBLANES)
                rows = slice(r0 + q * SUBLANES, r0 + q * SUBLANES + rb)
                win = zbuf[rows, lanes] if sft == 0 else zsh[sft - 1, rows, :]
                acc = acc + wdw_ref[k:k + 1, lanes] * win
            ybuf[r0:r0 + rb, lanes] = acc + bdw_ref[:, lanes]

    g2_b = _split_mod(mod_ref, seq_b)[5]
    xo_ref[...] = x1_scr[...] + g2_b * _swiglu(h2_scr[...], wg_ref, wu_ref, wd_ref)

    x1 = _conv_out(ybuf[...], x, g1, lng_ref[...], lnb_ref[...], wpw2_ref, bpw2_ref[...])
    x1_scr[...] = x1
    h2_scr[...] = _modulated_norm(x1, gffn_ref[...], sh2, sc2).astype(BF16)

    @pl.when(ta % tiles_per_seq == tiles_per_seq - 1)
    def _():
        cs_ref[...] = zbuf[tt + CARRY_SKEW:tt + CARRY_ROWS, :]

    zbuf[0:CARRY_ROWS, :] = zbuf[tt:tt + CARRY_ROWS, :]


def _l0_sample_kernel(x_ref, mod_ref, ctx_ref, gmix_ref, gffn_ref, wpw1_ref, bpw1_ref, wdw_ref, bdw_ref,
                      lng_ref, lnb_ref, wpw2_ref, bpw2_ref, wg_ref, wu_ref, wd_ref,
                      xo_ref, cso_ref, z_scr, y_scr):
    k = pl.program_id(0)

    @pl.when(k == 0)
    def _():
        sh1, sc1 = _split_mod(mod_ref)[:2]
        hb = _modulated_norm(x_ref[...], gmix_ref[...], sh1, sc1).astype(BF16)
        for j in range(D_MODEL // LANES):
            z_scr[:, j * LANES:(j + 1) * LANES] = _glu_column(hb, wpw1_ref, bpw1_ref, j)
        y_scr[...] = jnp.zeros(y_scr.shape, F32) + bdw_ref[...]

    @pl.when(k < CONV_CTX)
    def _():
        y_scr[...] += wdw_ref[pl.ds(k, 1), :] * ctx_ref[...]
        cso_ref[...] = ctx_ref[...]

    @pl.when(k == CONV_CTX)
    def _():
        z = z_scr[...]
        cso_ref[...] = z
        y = y_scr[...] + wdw_ref[CONV_CTX:CONV_WIDTH, :] * z
        _, _, g1, sh2, sc2, g2 = _split_mod(mod_ref)
        x1 = _conv_out(y, x_ref[...], g1, lng_ref[...], lnb_ref[...], wpw2_ref, bpw2_ref[...])
        hb = _modulated_norm(x1, gffn_ref[...], sh2, sc2).astype(BF16)
        xo_ref[...] = x1 + g2 * _swiglu(hb, wg_ref, wu_ref, wd_ref)


def _gmlp_in(x, sh1, sc1, gmix, win_ref, bin, lng, lnb):
    a = jax.nn.gelu(_dot(_modulated_norm(x, gmix, sh1, sc1), win_ref[...]) + bin)
    return a[:, :GMLP_WIDTH], _layernorm(a[:, GMLP_WIDTH:], lng, lnb)


def _l1_prompt_kernel(x_ref, mod_ref, gmix_ref, gffn_ref, gfin_ref, win_ref, bin_ref, lng_ref, lnb_ref,
                      ws_ref, bsf_ref, wout_ref, bout_ref, wg_ref, wu_ref, wd_ref,
                      yo_ref, um_scr, x1_scr, h2_scr, *, n_tiles, tiles_per_seq):
    c, gd = GMLP_CHUNK, GMLP_GROUP_DIM
    tt = x_ref.shape[0]
    s, _, seq_a, seq_b = _prompt_steps(n_tiles, tiles_per_seq)

    @pl.when(s == 0)
    def _():
        x1_scr[...] = jnp.zeros(x1_scr.shape, F32)
        h2_scr[...] = jnp.zeros(h2_scr.shape, BF16)

    x = x_ref[...]
    sh1, sc1, g1, sh2, sc2, _ = _split_mod(mod_ref, seq_a)
    u, v = _gmlp_in(x, sh1, sc1, gmix_ref[...], win_ref, bin_ref[...], lng_ref[...], lnb_ref[...])
    vb = v.astype(BF16)

    g2_b = _split_mod(mod_ref, seq_b)[5]
    x2 = x1_scr[...] + g2_b * _swiglu(h2_scr[...], wg_ref, wu_ref, wd_ref)
    yo_ref[...] = _rmsnorm(x2, gfin_ref[...])

    row = jax.lax.broadcasted_iota(jnp.int32, (c, c), 0)
    col = jax.lax.broadcasted_iota(jnp.int32, (c, c), 1)
    causal = row >= col
    for g in range(GMLP_GROUPS):
        cols = slice(g * gd, (g + 1) * gd)
        wt = jnp.where(causal, ws_ref[g], 0.0).astype(BF16)
        vg = jnp.concatenate([vb[r0:r0 + c, cols] for r0 in range(0, tt, c)], axis=1)
        m = jnp.dot(wt, vg, preferred_element_type=F32)
        for i, r0 in enumerate(range(0, tt, c)):
            mi = m[:, i * gd:(i + 1) * gd] + bsf_ref[:, cols]
            um_scr[r0:r0 + c, cols] = (u[r0:r0 + c, cols] * mi).astype(BF16)

    x1 = x + g1 * (jnp.dot(um_scr[...], wout_ref[...], preferred_element_type=F32) + bout_ref[...])
    x1_scr[...] = x1
    h2_scr[...] = _modulated_norm(x1, gffn_ref[...], sh2, sc2).astype(BF16)


def _l1_sample_kernel(x_ref, mod_ref, gmix_ref, gffn_ref, gfin_ref, win_ref, bin_ref, lng_ref, lnb_ref,
                      w00_ref, b0_ref, wout_ref, bout_ref, wg_ref, wu_ref, wd_ref, yo_ref, vo_ref):
    x = x_ref[...]
    sh1, sc1, g1, sh2, sc2, g2 = _split_mod(mod_ref)
    u, v = _gmlp_in(x, sh1, sc1, gmix_ref[...], win_ref, bin_ref[...], lng_ref[...], lnb_ref[...])
    vo_ref[...] = v
    um = u * (w00_ref[...] * v + b0_ref[...])
    x1 = x + g1 * (_dot(um, wout_ref[...]) + bout_ref[...])
    hb = _modulated_norm(x1, gffn_ref[...], sh2, sc2).astype(BF16)
    x2 = x1 + g2 * _swiglu(hb, wg_ref, wu_ref, wd_ref)
    yo_ref[...] = _rmsnorm(x2, gfin_ref[...])


class _Layer(NamedTuple):
    stacked: jax.Array
    layer: int


def _resident(c):
    if isinstance(c, _Layer):
        index = (c.layer,) + (0,) * (c.stacked.ndim - 1)
        return pl.BlockSpec((None,) + c.stacked.shape[1:], lambda *_: index, pipeline_mode=pl.Buffered(1))
    index = (0,) * c.ndim
    return pl.BlockSpec(c.shape, lambda *_: index, pipeline_mode=pl.Buffered(1))


def _operand(c):
    return c.stacked if isinstance(c, _Layer) else c


def _row(v):
    return v.reshape(1, -1)


def _params(n_grid):
    return pltpu.CompilerParams(dimension_semantics=("arbitrary",) * n_grid,
                                vmem_limit_bytes=VMEM_LIMIT_BYTES)


def _prompt_tiling(x, tt):
    b, t, d = x.shape
    assert t % tt == 0 and tt % GMLP_CHUNK == 0 and tt >= CARRY_ROWS
    n_tiles = b * t // tt
    mixer_tile = pl.BlockSpec((tt, d), lambda s: (jnp.minimum(s, n_tiles - 1), 0))
    ffn_tile = pl.BlockSpec((tt, d), lambda s: (jnp.maximum(s - 1, 0), 0))
    return x.reshape(b * t, d), mixer_tile, ffn_tile, dict(n_tiles=n_tiles, tiles_per_seq=t // tt)


def _cast_blocks(w, n_steps):
    rows, cols = w.stacked.shape[1:]
    blk = next(r for r in range(BF16_SUBLANES, rows + 1, BF16_SUBLANES) if rows % r == 0 and rows // r <= n_steps)
    last = rows // blk - 1
    src = pl.BlockSpec((None, blk, cols), lambda s: (w.layer, jnp.minimum(s, last), 0))
    dst = pl.BlockSpec((blk, cols), lambda s: (jnp.minimum(s, last), 0))
    return src, dst, jax.ShapeDtypeStruct((rows, cols), BF16)


def _l0_prompt_call(x, mod, consts, cast):
    b, t, d = x.shape
    tt = L0_PROMPT_TILE
    x2d, mixer_tile, ffn_tile, counts = _prompt_tiling(x, tt)
    n_tiles, tiles_per_seq = counts["n_tiles"], counts["tiles_per_seq"]
    assert len(consts) == N_L0_CONSTS
    state = pl.BlockSpec((None, None, CONV_CTX, d),
                         lambda s: (0, jnp.minimum(s, n_tiles - 1) // tiles_per_seq, 0, 0))
    cast_src, cast_dst, cast_shapes = zip(*[_cast_blocks(w, n_tiles + 1) for w in cast])
    xo, cs, *cast_out = pl.pallas_call(
        functools.partial(_l0_prompt_kernel, n_cast=len(cast), **counts),
        grid=(n_tiles + 1,),
        in_specs=[mixer_tile, _resident(mod)] + [_resident(c) for c in consts] + list(cast_src),
        out_specs=[ffn_tile, state] + list(cast_dst),
        out_shape=[jax.ShapeDtypeStruct((b * t, d), F32), jax.ShapeDtypeStruct((1, b, CONV_CTX, d), F32)]
        + list(cast_shapes),
        scratch_shapes=[pltpu.VMEM((CARRY_ROWS + tt, d), F32),
                        pltpu.VMEM((SUBLANES - 1, CARRY_ROWS + tt - SUBLANES, LANES), F32),
                        pltpu.VMEM((tt, d), F32),
                        pltpu.VMEM((tt, d), F32),
                        pltpu.VMEM((tt, d), BF16)],
        compiler_params=_params(1),
        name="l0_prompt",
    )(x2d, _operand(mod), *map(_operand, consts), *map(_operand, cast))
    return xo.reshape(b, t, d), cs, cast_out


def _l0_sample_call(x, mod, ctx_rows, consts):
    n, d = x.shape
    row_in = pl.BlockSpec((None, None, n, d), lambda k: (0, jnp.minimum(k, CONV_CTX - 1), 0, 0))
    row_out = pl.BlockSpec((None, None, n, d), lambda k: (0, jnp.maximum(k - 1, 0), 0, 0))
    return pl.pallas_call(
        _l0_sample_kernel,
        grid=(CONV_WIDTH,),
        in_specs=[_resident(x), _resident(mod), row_in] + [_resident(c) for c in consts],
        out_specs=[pl.BlockSpec((n, d), lambda k: (0, 0)), row_out],
        out_shape=[jax.ShapeDtypeStruct((n, d), F32), jax.ShapeDtypeStruct(ctx_rows.shape, F32)],
        scratch_shapes=[pltpu.VMEM((n, d), F32), pltpu.VMEM((n, d), F32)],
        compiler_params=_params(1),
        name="l0_sample",
    )(x, _operand(mod), ctx_rows, *map(_operand, consts))


def _l1_prompt_call(x, mod, consts):
    b, t, d = x.shape
    tt = L1_PROMPT_TILE
    x2d, mixer_tile, ffn_tile, counts = _prompt_tiling(x, tt)
    y = pl.pallas_call(
        functools.partial(_l1_prompt_kernel, **counts),
        grid=(counts["n_tiles"] + 1,),
        in_specs=[mixer_tile, _resident(mod)] + [_resident(c) for c in consts],
        out_specs=ffn_tile,
        out_shape=jax.ShapeDtypeStruct((b * t, d), F32),
        scratch_shapes=[pltpu.VMEM((tt, GMLP_WIDTH), BF16),
                        pltpu.VMEM((tt, d), F32),
                        pltpu.VMEM((tt, d), BF16)],
        compiler_params=_params(1),
        name="l1_prompt",
    )(x2d, _operand(mod), *map(_operand, consts))
    return y.reshape(b, t, d)


def _l1_sample_call(x, mod, consts):
    n, d = x.shape
    return pl.pallas_call(
        _l1_sample_kernel,
        grid=(1,),
        in_specs=[_resident(x), _resident(mod)] + [_resident(c) for c in consts],
        out_specs=[pl.BlockSpec((n, d), lambda i: (0, 0)), pl.BlockSpec((n, GMLP_WIDTH), lambda i: (0, 0))],
        out_shape=[jax.ShapeDtypeStruct((n, d), F32), jax.ShapeDtypeStruct((n, GMLP_WIDTH), F32)],
        compiler_params=_params(1),
        name="l1_sample",
    )(x, _operand(mod), *map(_operand, consts))


def kernel(x_prompt, x_sample, c_prompt, c_sample, state_conv, w_ada, b_ada, norm_mix_g, norm_ffn_g, final_norm_g, conv_w_pw1, conv_b_pw1, conv_w_dw, conv_b_dw, conv_ln_g, conv_ln_b, conv_w_pw2, conv_b_pw2, gmlp_w_in, gmlp_b_in, gmlp_ln_g, gmlp_ln_b, gmlp_w_s, gmlp_b_s, gmlp_w_out, gmlp_b_out, ffn_w_gate, ffn_w_up, ffn_w_down):
    d = x_prompt.shape[-1]
    ns = x_sample.shape[0]
    assert x_sample.shape[1] == 1 and state_conv.shape[0] == 1 and gmlp_w_in.shape[0] == 1

    mod_p, mod_s = _ada_call(c_prompt, c_sample, w_ada, b_ada)

    l0 = (_row(norm_mix_g[0]), _row(norm_ffn_g[0]), _Layer(_pair_glu_columns(conv_w_pw1).astype(BF16), 0),
          _pair_glu_columns(conv_b_pw1),
          _Layer(conv_w_dw, 0), _row(conv_b_dw[0]), _row(conv_ln_g[0]), _row(conv_ln_b[0]),
          _Layer(conv_w_pw2.astype(BF16), 0), _row(conv_b_pw2[0]),
          ffn_w_gate[0].astype(BF16), ffn_w_up[0].astype(BF16), ffn_w_down[0].astype(BF16))

    l1_f32 = (_Layer(gmlp_w_in, 0), _Layer(gmlp_w_out, 0),
              _Layer(ffn_w_gate, 1), _Layer(ffn_w_up, 1), _Layer(ffn_w_down, 1))
    xp, cs_p, (w_in, w_out, w_gate, w_up, w_down) = _l0_prompt_call(x_prompt, _Layer(mod_p, 0), l0, l1_f32)

    l1_head = (_row(norm_mix_g[1]), _row(norm_ffn_g[1]), _row(final_norm_g), w_in,
               _row(gmlp_b_in[0]), _row(gmlp_ln_g[0]), _row(gmlp_ln_b[0]))
    l1_tail = (w_out, _row(gmlp_b_out[0]), w_gate, w_up, w_down)

    bias_full = jnp.repeat(gmlp_b_s[0].T, GMLP_GROUP_DIM, axis=1)
    w00 = _row(jnp.repeat(gmlp_w_s[0][:, 0, 0], GMLP_GROUP_DIM))
    b0 = _row(jnp.repeat(gmlp_b_s[0][:, 0], GMLP_GROUP_DIM))

    y_prompt = _l1_prompt_call(xp, _Layer(mod_p, 1), l1_head + (_Layer(gmlp_w_s, 0), bias_full) + l1_tail)

    xs, cs_rows = _l0_sample_call(x_sample.reshape(ns, d), _Layer(mod_s, 0), jnp.swapaxes(state_conv, 1, 2), l0)
    y_sample, v_s = _l1_sample_call(xs, _Layer(mod_s, 1), l1_head + (w00, b0) + l1_tail)

    return (y_prompt, y_sample.reshape(ns, 1, d), cs_p, jnp.swapaxes(cs_rows, 1, 2),
            v_s.reshape(1, ns, 1, GMLP_WIDTH))
```

```python
import functools
from typing import NamedTuple

import jax
import jax.numpy as jnp
from jax.experimental import pallas as pl
from jax.experimental.pallas import tpu as pltpu

D_MODEL = 1024
CONV_WIDTH = 31
CONV_CTX = CONV_WIDTH - 1
GMLP_CHUNK = 128
GMLP_WIDTH = 2 * D_MODEL
GMLP_GROUPS = 8
GMLP_GROUP_DIM = GMLP_WIDTH // GMLP_GROUPS
EPS = 1e-6

SUBLANES = 8
BF16_SUBLANES = 16
LANES = 128
CARRY_ROWS = 32
CARRY_SKEW = CARRY_ROWS - CONV_CTX
L0_PROMPT_TILE = 256
L1_PROMPT_TILE = 512
CONV_ROW_BLOCK = 64
ADA_TILE_N = 1536
SAMPLE_TAPS_PER_STEP = 2
VMEM_LIMIT_BYTES = 60 * 1024 * 1024

F32 = jnp.float32
BF16 = jnp.bfloat16


def _dot(a, b):
    return jnp.dot(a.astype(BF16), b, preferred_element_type=F32)


def _rmsnorm(x, g):
    return x * jax.lax.rsqrt(jnp.mean(x * x, axis=-1, keepdims=True) + EPS) * g


def _layernorm(x, g, b):
    mu = jnp.mean(x, axis=-1, keepdims=True)
    xc = x - mu
    var = jnp.mean(xc * xc, axis=-1, keepdims=True)
    return xc * jax.lax.rsqrt(var + EPS) * g + b


def _silu(x):
    return x * jax.nn.sigmoid(x)


def _modulated_norm(x, g, shift, scale):
    return _rmsnorm(x, g) * (1.0 + scale) + shift


def _swiglu(hb, wg_ref, wu_ref, wd_ref):
    gate = jnp.dot(hb, wg_ref[...], preferred_element_type=F32)
    up = jnp.dot(hb, wu_ref[...], preferred_element_type=F32)
    return _dot(_silu(gate) * up, wd_ref[...])


def _split_mod(mod_ref, row=None):
    d = D_MODEL
    rows = slice(None) if row is None else pl.ds(row, 1)
    return [mod_ref[rows, i * d:(i + 1) * d] for i in range(6)]


def _ada_kernel(cp_ref, cs_ref, w_ref, b_ref, op_ref, os_ref):
    w = w_ref[...].astype(BF16)
    op_ref[...] = _dot(_silu(cp_ref[...]), w) + b_ref[...]
    os_ref[...] = _dot(_silu(cs_ref[...]), w) + b_ref[...]


def _ada_call(c_prompt, c_sample, w_ada, b_ada):
    depth, d, n = w_ada.shape
    col_tile = lambda rows: pl.BlockSpec((None, rows, ADA_TILE_N), lambda i, j: (i, 0, j))
    return pl.pallas_call(
        _ada_kernel,
        grid=(depth, n // ADA_TILE_N),
        in_specs=[
            pl.BlockSpec(c_prompt.shape, lambda i, j: (0, 0)),
            pl.BlockSpec(c_sample.shape, lambda i, j: (0, 0)),
            col_tile(d),
            col_tile(1),
        ],
        out_specs=[col_tile(c_prompt.shape[0]), col_tile(c_sample.shape[0])],
        out_shape=[jax.ShapeDtypeStruct((depth, c_prompt.shape[0], n), F32),
                   jax.ShapeDtypeStruct((depth, c_sample.shape[0], n), F32)],
        compiler_params=pltpu.CompilerParams(dimension_semantics=("arbitrary", "arbitrary")),
        name="adaln_mod",
    )(c_prompt, c_sample, w_ada, b_ada.reshape(depth, 1, n))


def _glu_column(hb, wpw1_ref, bpw1_ref, j):
    value, gate = slice(j * LANES, (j + 1) * LANES), slice(D_MODEL + j * LANES, D_MODEL + (j + 1) * LANES)
    w = jnp.concatenate([wpw1_ref[:, value], wpw1_ref[:, gate]], axis=1)
    b = jnp.concatenate([bpw1_ref[:, value], bpw1_ref[:, gate]], axis=1)
    a = jnp.dot(hb, w, preferred_element_type=F32) + b
    return a[:, :LANES] * jax.nn.sigmoid(a[:, LANES:])


def _conv_out(y, x, g1, lng, lnb, wpw2_ref, bpw2):
    y = _silu(_layernorm(y, lng, lnb))
    return x + g1 * (_dot(y, wpw2_ref[...]) + bpw2)


def _prompt_steps(n_tiles, tiles_per_seq):
    s = pl.program_id(0)
    ta = jnp.minimum(s, n_tiles - 1)
    tb = jnp.maximum(s - 1, 0)
    return s, ta, ta // tiles_per_seq, tb // tiles_per_seq


N_L0_CONSTS = 13


def _l0_prompt_kernel(x_ref, mod_ref, *refs, n_tiles, tiles_per_seq, n_cast):
    (gmix_ref, gffn_ref, wpw1_ref, bpw1_ref, wdw_ref, bdw_ref, lng_ref, lnb_ref, wpw2_ref, bpw2_ref,
     wg_ref, wu_ref, wd_ref) = refs[:N_L0_CONSTS]
    cast_src = refs[N_L0_CONSTS:N_L0_CONSTS + n_cast]
    xo_ref, cs_ref = refs[N_L0_CONSTS + n_cast:N_L0_CONSTS + n_cast + 2]
    cast_dst = refs[N_L0_CONSTS + n_cast + 2:N_L0_CONSTS + 2 * n_cast + 2]
    zbuf, zsh, ybuf, x1_scr, h2_scr = refs[N_L0_CONSTS + 2 * n_cast + 2:]
    d = D_MODEL
    tt = x_ref.shape[0]
    s, ta, seq_a, seq_b = _prompt_steps(n_tiles, tiles_per_seq)

    @pl.when(s == 0)
    def _():
        x1_scr[...] = jnp.zeros(x1_scr.shape, F32)
        h2_scr[...] = jnp.zeros(h2_scr.shape, BF16)

    @pl.when(ta % tiles_per_seq == 0)
    def _():
        zbuf[0:CARRY_ROWS, :] = jnp.zeros((CARRY_ROWS, d), F32)

    for src, dst in zip(cast_src, cast_dst):
        dst[...] = src[...].astype(BF16)

    x = x_ref[...]
    sh1, sc1, g1, sh2, sc2, _ = _split_mod(mod_ref, seq_a)
    hb = _modulated_norm(x, gmix_ref[...], sh1, sc1).astype(BF16)

    n_sh = tt + CARRY_ROWS - SUBLANES
    rb = CONV_ROW_BLOCK
    for j in range(d // LANES):
        lanes = slice(j * LANES, (j + 1) * LANES)
        zbuf[CARRY_ROWS:CARRY_ROWS + tt, lanes] = _glu_column(hb, wpw1_ref, bpw1_ref, j)
        for sft in range(1, SUBLANES):
            zsh[sft - 1, :, :] = zbuf[sft:sft + n_sh, lanes]
        for r0 in range(0, tt, rb):
            acc = jnp.zeros((rb, LANES), F32)
            for k in range(CONV_WIDTH):
                q, sft = divmod(CARRY_SKEW + k, SUBLANES)
                rows = slice(r0 + q * SUBLANES, r0 + q * SUBLANES + rb)
                win = zbuf[rows, lanes] if sft == 0 else zsh[sft - 1, rows, :]
                acc = acc + wdw_ref[k:k + 1, lanes] * win
            ybuf[r0:r0 + rb, lanes] = acc + bdw_ref[:, lanes]

    g2_b = _split_mod(mod_ref, seq_b)[5]
    xo_ref[...] = x1_scr[...] + g2_b * _swiglu(h2_scr[...], wg_ref, wu_ref, wd_ref)

    x1 = _conv_out(ybuf[...], x, g1, lng_ref[...], lnb_ref[...], wpw2_ref, bpw2_ref[...])
    x1_scr[...] = x1
    h2_scr[...] = _modulated_norm(x1, gffn_ref[...], sh2, sc2).astype(BF16)

    @pl.when(ta % tiles_per_seq == tiles_per_seq - 1)
    def _():
        cs_ref[...] = zbuf[tt + CARRY_SKEW:tt + CARRY_ROWS, :]

    zbuf[0:CARRY_ROWS, :] = zbuf[tt:tt + CARRY_ROWS, :]


def _l0_sample_kernel(x_ref, mod_ref, ctx_ref, nxt_ref, gmix_ref, gffn_ref, wpw1_ref, bpw1_ref, wdw_ref, bdw_ref,
                      lng_ref, lnb_ref, wpw2_ref, bpw2_ref, wg_ref, wu_ref, wd_ref,
                      xo_ref, cso_ref, z_scr, y_scr):
    j = pl.program_id(0)
    last = pl.num_programs(0) - 1

    @pl.when(j == 0)
    def _():
        sh1, sc1 = _split_mod(mod_ref)[:2]
        hb = _modulated_norm(x_ref[...], gmix_ref[...], sh1, sc1).astype(BF16)
        for col in range(D_MODEL // LANES):
            z_scr[:, col * LANES:(col + 1) * LANES] = _glu_column(hb, wpw1_ref, bpw1_ref, col)
        y_scr[...] = jnp.zeros(y_scr.shape, F32) + bdw_ref[...]

    y_scr[...] += wdw_ref[pl.ds(2 * j, 1), :] * ctx_ref[0] + wdw_ref[pl.ds(2 * j + 1, 1), :] * ctx_ref[1]
    cso_ref[0] = ctx_ref[1]

    @pl.when(j < last)
    def _():
        cso_ref[1] = nxt_ref[...]

    @pl.when(j == last)
    def _():
        z = z_scr[...]
        cso_ref[1] = z
        y = y_scr[...] + wdw_ref[CONV_CTX:CONV_WIDTH, :] * z
        _, _, g1, sh2, sc2, g2 = _split_mod(mod_ref)
        x1 = _conv_out(y, x_ref[...], g1, lng_ref[...], lnb_ref[...], wpw2_ref, bpw2_ref[...])
        hb = _modulated_norm(x1, gffn_ref[...], sh2, sc2).astype(BF16)
        xo_ref[...] = x1 + g2 * _swiglu(hb, wg_ref, wu_ref, wd_ref)


def _gmlp_in(x, sh1, sc1, gmix, win_ref, bin, lng, lnb):
    a = jax.nn.gelu(_dot(_modulated_norm(x, gmix, sh1, sc1), win_ref[...]) + bin)
    return a[:, :GMLP_WIDTH], _layernorm(a[:, GMLP_WIDTH:], lng, lnb)


def _l1_prompt_kernel(x_ref, mod_ref, gmix_ref, gffn_ref, gfin_ref, win_ref, bin_ref, lng_ref, lnb_ref,
                      ws_ref, bsf_ref, wout_ref, bout_ref, wg_ref, wu_ref, wd_ref,
                      yo_ref, um_scr, x1_scr, h2_scr, *, n_tiles, tiles_per_seq):
    c, gd = GMLP_CHUNK, GMLP_GROUP_DIM
    tt = x_ref.shape[0]
    s, _, seq_a, seq_b = _prompt_steps(n_tiles, tiles_per_seq)

    @pl.when(s == 0)
    def _():
        x1_scr[...] = jnp.zeros(x1_scr.shape, F32)
        h2_scr[...] = jnp.zeros(h2_scr.shape, BF16)

    x = x_ref[...]
    sh1, sc1, g1, sh2, sc2, _ = _split_mod(mod_ref, seq_a)
    u, v = _gmlp_in(x, sh1, sc1, gmix_ref[...], win_ref, bin_ref[...], lng_ref[...], lnb_ref[...])
    vb = v.astype(BF16)

    g2_b = _split_mod(mod_ref, seq_b)[5]
    x2 = x1_scr[...] + g2_b * _swiglu(h2_scr[...], wg_ref, wu_ref, wd_ref)
    yo_ref[...] = _rmsnorm(x2, gfin_ref[...])

    row = jax.lax.broadcasted_iota(jnp.int32, (c, c), 0)
    col = jax.lax.broadcasted_iota(jnp.int32, (c, c), 1)
    causal = row >= col
    for g in range(GMLP_GROUPS):
        cols = slice(g * gd, (g + 1) * gd)
        wt = jnp.where(causal, ws_ref[g], 0.0).astype(BF16)
        vg = jnp.concatenate([vb[r0:r0 + c, cols] for r0 in range(0, tt, c)], axis=1)
        m = jnp.dot(wt, vg, preferred_element_type=F32)
        for i, r0 in enumerate(range(0, tt, c)):
            mi = m[:, i * gd:(i + 1) * gd] + bsf_ref[:, cols]
            um_scr[r0:r0 + c, cols] = (u[r0:r0 + c, cols] * mi).astype(BF16)

    x1 = x + g1 * (jnp.dot(um_scr[...], wout_ref[...], preferred_element_type=F32) + bout_ref[...])
    x1_scr[...] = x1
    h2_scr[...] = _modulated_norm(x1, gffn_ref[...], sh2, sc2).astype(BF16)


def _l1_sample_kernel(x_ref, mod_ref, gmix_ref, gffn_ref, gfin_ref, win_ref, bin_ref, lng_ref, lnb_ref,
                      w00_ref, b0_ref, wout_ref, bout_ref, wg_ref, wu_ref, wd_ref, yo_ref, vo_ref):
    x = x_ref[...]
    sh1, sc1, g1, sh2, sc2, g2 = _split_mod(mod_ref)
    u, v = _gmlp_in(x, sh1, sc1, gmix_ref[...], win_ref, bin_ref[...], lng_ref[...], lnb_ref[...])
    vo_ref[...] = v
    um = u * (w00_ref[...] * v + b0_ref[...])
    x1 = x + g1 * (_dot(um, wout_ref[...]) + bout_ref[...])
    hb = _modulated_norm(x1, gffn_ref[...], sh2, sc2).astype(BF16)
    x2 = x1 + g2 * _swiglu(hb, wg_ref, wu_ref, wd_ref)
    yo_ref[...] = _rmsnorm(x2, gfin_ref[...])


class _Layer(NamedTuple):
    stacked: jax.Array
    layer: int


def _resident(c):
    if isinstance(c, _Layer):
        index = (c.layer,) + (0,) * (c.stacked.ndim - 1)
        return pl.BlockSpec((None,) + c.stacked.shape[1:], lambda *_: index, pipeline_mode=pl.Buffered(1))
    index = (0,) * c.ndim
    return pl.BlockSpec(c.shape, lambda *_: index, pipeline_mode=pl.Buffered(1))


def _operand(c):
    return c.stacked if isinstance(c, _Layer) else c


def _row(v):
    return v.reshape(1, -1)


def _params(n_grid):
    return pltpu.CompilerParams(dimension_semantics=("arbitrary",) * n_grid,
                                vmem_limit_bytes=VMEM_LIMIT_BYTES)


def _prompt_tiling(x, tt):
    b, t, d = x.shape
    assert t % tt == 0 and tt % GMLP_CHUNK == 0 and tt >= CARRY_ROWS
    n_tiles = b * t // tt
    mixer_tile = pl.BlockSpec((tt, d), lambda s: (jnp.minimum(s, n_tiles - 1), 0))
    ffn_tile = pl.BlockSpec((tt, d), lambda s: (jnp.maximum(s - 1, 0), 0))
    return x.reshape(b * t, d), mixer_tile, ffn_tile, dict(n_tiles=n_tiles, tiles_per_seq=t // tt)


def _cast_blocks(w, n_steps):
    rows, cols = w.stacked.shape[1:]
    blk = next(r for r in range(BF16_SUBLANES, rows + 1, BF16_SUBLANES) if rows % r == 0 and rows // r <= n_steps)
    last = rows // blk - 1
    src = pl.BlockSpec((None, blk, cols), lambda s: (w.layer, jnp.minimum(s, last), 0))
    dst = pl.BlockSpec((blk, cols), lambda s: (jnp.minimum(s, last), 0))
    return src, dst, jax.ShapeDtypeStruct((rows, cols), BF16)


def _l0_prompt_call(x, mod, consts, cast):
    b, t, d = x.shape
    tt = L0_PROMPT_TILE
    x2d, mixer_tile, ffn_tile, counts = _prompt_tiling(x, tt)
    n_tiles, tiles_per_seq = counts["n_tiles"], counts["tiles_per_seq"]
    assert len(consts) == N_L0_CONSTS
    state = pl.BlockSpec((None, None, CONV_CTX, d),
                         lambda s: (0, jnp.minimum(s, n_tiles - 1) // tiles_per_seq, 0, 0))
    cast_src, cast_dst, cast_shapes = zip(*[_cast_blocks(w, n_tiles + 1) for w in cast])
    xo, cs, *cast_out = pl.pallas_call(
        functools.partial(_l0_prompt_kernel, n_cast=len(cast), **counts),
        grid=(n_tiles + 1,),
        in_specs=[mixer_tile, _resident(mod)] + [_resident(c) for c in consts] + list(cast_src),
        out_specs=[ffn_tile, state] + list(cast_dst),
        out_shape=[jax.ShapeDtypeStruct((b * t, d), F32), jax.ShapeDtypeStruct((1, b, CONV_CTX, d), F32)]
        + list(cast_shapes),
        scratch_shapes=[pltpu.VMEM((CARRY_ROWS + tt, d), F32),
                        pltpu.VMEM((SUBLANES - 1, CARRY_ROWS + tt - SUBLANES, LANES), F32),
                        pltpu.VMEM((tt, d), F32),
                        pltpu.VMEM((tt, d), F32),
                        pltpu.VMEM((tt, d), BF16)],
        compiler_params=_params(1),
        name="l0_prompt",
    )(x2d, _operand(mod), *map(_operand, consts), *map(_operand, cast))
    return xo.reshape(b, t, d), cs, cast_out


def _l0_sample_call(x, mod, ctx_rows, consts):
    n, d = x.shape
    assert CONV_CTX % SAMPLE_TAPS_PER_STEP == 0
    pair = pl.BlockSpec((None, SAMPLE_TAPS_PER_STEP, n, d), lambda j: (0, j, 0, 0))
    nxt = pl.BlockSpec((None, None, n, d),
                       lambda j: (0, jnp.minimum(SAMPLE_TAPS_PER_STEP * (j + 1), CONV_CTX - 1), 0, 0))
    return pl.pallas_call(
        _l0_sample_kernel,
        grid=(CONV_CTX // SAMPLE_TAPS_PER_STEP,),
        in_specs=[_resident(x), _resident(mod), pair, nxt] + [_resident(c) for c in consts],
        out_specs=[pl.BlockSpec((n, d), lambda j: (0, 0)), pair],
        out_shape=[jax.ShapeDtypeStruct((n, d), F32), jax.ShapeDtypeStruct(ctx_rows.shape, F32)],
        scratch_shapes=[pltpu.VMEM((n, d), F32), pltpu.VMEM((n, d), F32)],
        compiler_params=_params(1),
        name="l0_sample",
    )(x, _operand(mod), ctx_rows, ctx_rows, *map(_operand, consts))


def _l1_prompt_call(x, mod, consts):
    b, t, d = x.shape
    tt = L1_PROMPT_TILE
    x2d, mixer_tile, ffn_tile, counts = _prompt_tiling(x, tt)
    y = pl.pallas_call(
        functools.partial(_l1_prompt_kernel, **counts),
        grid=(counts["n_tiles"] + 1,),
        in_specs=[mixer_tile, _resident(mod)] + [_resident(c) for c in consts],
        out_specs=ffn_tile,
        out_shape=jax.ShapeDtypeStruct((b * t, d), F32),
        scratch_shapes=[pltpu.VMEM((tt, GMLP_WIDTH), BF16),
                        pltpu.VMEM((tt, d), F32),
                        pltpu.VMEM((tt, d), BF16)],
        compiler_params=_params(1),
        name="l1_prompt",
    )(x2d, _operand(mod), *map(_operand, consts))
    return y.reshape(b, t, d)


def _l1_sample_call(x, mod, consts):
    n, d = x.shape
    return pl.pallas_call(
        _l1_sample_kernel,
        grid=(1,),
        in_specs=[_resident(x), _resident(mod)] + [_resident(c) for c in consts],
        out_specs=[pl.BlockSpec((n, d), lambda i: (0, 0)), pl.BlockSpec((n, GMLP_WIDTH), lambda i: (0, 0))],
        out_shape=[jax.ShapeDtypeStruct((n, d), F32), jax.ShapeDtypeStruct((n, GMLP_WIDTH), F32)],
        compiler_params=_params(1),
        name="l1_sample",
    )(x, _operand(mod), *map(_operand, consts))


def kernel(x_prompt, x_sample, c_prompt, c_sample, state_conv, w_ada, b_ada, norm_mix_g, norm_ffn_g, final_norm_g, conv_w_pw1, conv_b_pw1, conv_w_dw, conv_b_dw, conv_ln_g, conv_ln_b, conv_w_pw2, conv_b_pw2, gmlp_w_in, gmlp_b_in, gmlp_ln_g, gmlp_ln_b, gmlp_w_s, gmlp_b_s, gmlp_w_out, gmlp_b_out, ffn_w_gate, ffn_w_up, ffn_w_down):
    d = x_prompt.shape[-1]
    ns = x_sample.shape[0]
    assert x_sample.shape[1] == 1 and state_conv.shape[0] == 1 and gmlp_w_in.shape[0] == 1

    mod_p, mod_s = _ada_call(c_prompt, c_sample, w_ada, b_ada)

    l0 = (_row(norm_mix_g[0]), _row(norm_ffn_g[0]), _Layer(conv_w_pw1.astype(BF16), 0), _row(conv_b_pw1[0]),
          _Layer(conv_w_dw, 0), _row(conv_b_dw[0]), _row(conv_ln_g[0]), _row(conv_ln_b[0]),
          _Layer(conv_w_pw2.astype(BF16), 0), _row(conv_b_pw2[0]),
          ffn_w_gate[0].astype(BF16), ffn_w_up[0].astype(BF16), ffn_w_down[0].astype(BF16))

    l1_f32 = (_Layer(gmlp_w_in, 0), _Layer(gmlp_w_out, 0),
              _Layer(ffn_w_gate, 1), _Layer(ffn_w_up, 1), _Layer(ffn_w_down, 1))
    xp, cs_p, (w_in, w_out, w_gate, w_up, w_down) = _l0_prompt_call(x_prompt, _Layer(mod_p, 0), l0, l1_f32)

    l1_head = (_row(norm_mix_g[1]), _row(norm_ffn_g[1]), _row(final_norm_g), w_in,
               _row(gmlp_b_in[0]), _row(gmlp_ln_g[0]), _row(gmlp_ln_b[0]))
    l1_tail = (w_out, _row(gmlp_b_out[0]), w_gate, w_up, w_down)

    bias_full = jnp.repeat(gmlp_b_s[0].T, GMLP_GROUP_DIM, axis=1)
    w00 = _row(jnp.repeat(gmlp_w_s[0][:, 0, 0], GMLP_GROUP_DIM))
    b0 = _row(jnp.repeat(gmlp_b_s[0][:, 0], GMLP_GROUP_DIM))

    y_prompt = _l1_prompt_call(xp, _Layer(mod_p, 1), l1_head + (_Layer(gmlp_w_s, 0), bias_full) + l1_tail)

    xs, cs_rows = _l0_sample_call(x_sample.reshape(ns, d), _Layer(mod_s, 0), jnp.swapaxes(state_conv, 1, 2), l0)
    y_sample, v_s = _l1_sample_call(xs, _Layer(mod_s, 1), l1_head + (w00, b0) + l1_tail)

    return (y_prompt, y_sample.reshape(ns, 1, d), cs_p, jnp.swapaxes(cs_rows, 1, 2),
            v_s.reshape(1, ns, 1, GMLP_WIDTH))
```

```python
import functools
from typing import NamedTuple

import jax
import jax.numpy as jnp
from jax.experimental import pallas as pl
from jax.experimental.pallas import tpu as pltpu

D_MODEL = 1024
CONV_WIDTH = 31
CONV_CTX = CONV_WIDTH - 1
GMLP_CHUNK = 128
GMLP_WIDTH = 2 * D_MODEL
GMLP_GROUPS = 8
GMLP_GROUP_DIM = GMLP_WIDTH // GMLP_GROUPS
EPS = 1e-6

SUBLANES = 8
BF16_SUBLANES = 16
LANES = 128
CARRY_ROWS = 32
CARRY_SKEW = CARRY_ROWS - CONV_CTX
L0_PROMPT_TILE = 256
L1_PROMPT_TILE = 512
CONV_ROW_BLOCK = 64
FFN_PART_AFTER_COLUMNS = (2, 4, 6, 7)
ADA_TILE_N = 1536
SAMPLE_TAPS_PER_STEP = 2
VMEM_LIMIT_BYTES = 60 * 1024 * 1024

F32 = jnp.float32
BF16 = jnp.bfloat16


def _dot(a, b):
    return jnp.dot(a.astype(BF16), b, preferred_element_type=F32)


def _rmsnorm(x, g):
    return x * jax.lax.rsqrt(jnp.mean(x * x, axis=-1, keepdims=True) + EPS) * g


def _layernorm(x, g, b):
    mu = jnp.mean(x, axis=-1, keepdims=True)
    xc = x - mu
    var = jnp.mean(xc * xc, axis=-1, keepdims=True)
    return xc * jax.lax.rsqrt(var + EPS) * g + b


def _silu(x):
    return x * jax.nn.sigmoid(x)


def _modulated_norm(x, g, shift, scale):
    return _rmsnorm(x, g) * (1.0 + scale) + shift


def _swiglu(hb, wg_ref, wu_ref, wd_ref):
    gate = jnp.dot(hb, wg_ref[...], preferred_element_type=F32)
    up = jnp.dot(hb, wu_ref[...], preferred_element_type=F32)
    return _dot(_silu(gate) * up, wd_ref[...])


def _split_mod(mod_ref, row=None):
    d = D_MODEL
    rows = slice(None) if row is None else pl.ds(row, 1)
    return [mod_ref[rows, i * d:(i + 1) * d] for i in range(6)]


def _ada_kernel(cp_ref, cs_ref, w_ref, b_ref, op_ref, os_ref):
    w = w_ref[...].astype(BF16)
    op_ref[...] = _dot(_silu(cp_ref[...]), w) + b_ref[...]
    os_ref[...] = _dot(_silu(cs_ref[...]), w) + b_ref[...]


def _ada_call(c_prompt, c_sample, w_ada, b_ada):
    depth, d, n = w_ada.shape
    col_tile = lambda rows: pl.BlockSpec((None, rows, ADA_TILE_N), lambda i, j: (i, 0, j))
    return pl.pallas_call(
        _ada_kernel,
        grid=(depth, n // ADA_TILE_N),
        in_specs=[
            pl.BlockSpec(c_prompt.shape, lambda i, j: (0, 0)),
            pl.BlockSpec(c_sample.shape, lambda i, j: (0, 0)),
            col_tile(d),
            col_tile(1),
        ],
        out_specs=[col_tile(c_prompt.shape[0]), col_tile(c_sample.shape[0])],
        out_shape=[jax.ShapeDtypeStruct((depth, c_prompt.shape[0], n), F32),
                   jax.ShapeDtypeStruct((depth, c_sample.shape[0], n), F32)],
        compiler_params=pltpu.CompilerParams(dimension_semantics=("arbitrary", "arbitrary")),
        name="adaln_mod",
    )(c_prompt, c_sample, w_ada, b_ada.reshape(depth, 1, n))


def _glu_column(hb, wpw1_ref, bpw1_ref, j):
    value, gate = slice(j * LANES, (j + 1) * LANES), slice(D_MODEL + j * LANES, D_MODEL + (j + 1) * LANES)
    w = jnp.concatenate([wpw1_ref[:, value], wpw1_ref[:, gate]], axis=1)
    b = jnp.concatenate([bpw1_ref[:, value], bpw1_ref[:, gate]], axis=1)
    a = jnp.dot(hb, w, preferred_element_type=F32) + b
    return a[:, :LANES] * jax.nn.sigmoid(a[:, LANES:])


def _conv_out(y, x, g1, lng, lnb, wpw2_ref, bpw2):
    y = _silu(_layernorm(y, lng, lnb))
    return x + g1 * (_dot(y, wpw2_ref[...]) + bpw2)


def _prompt_steps(n_tiles, tiles_per_seq):
    s = pl.program_id(0)
    ta = jnp.minimum(s, n_tiles - 1)
    tb = jnp.maximum(s - 1, 0)
    return s, ta, ta // tiles_per_seq, tb // tiles_per_seq


N_L0_CONSTS = 13


def _l0_prompt_kernel(x_ref, mod_ref, *refs, n_tiles, tiles_per_seq, n_cast):
    (gmix_ref, gffn_ref, wpw1_ref, bpw1_ref, wdw_ref, bdw_ref, lng_ref, lnb_ref, wpw2_ref, bpw2_ref,
     wg_ref, wu_ref, wd_ref) = refs[:N_L0_CONSTS]
    cast_src = refs[N_L0_CONSTS:N_L0_CONSTS + n_cast]
    xo_ref, cs_ref = refs[N_L0_CONSTS + n_cast:N_L0_CONSTS + n_cast + 2]
    cast_dst = refs[N_L0_CONSTS + n_cast + 2:N_L0_CONSTS + 2 * n_cast + 2]
    zbuf, zsh, ybuf, x1_scr, h2_scr = refs[N_L0_CONSTS + 2 * n_cast + 2:]
    d = D_MODEL
    tt = x_ref.shape[0]
    s, ta, seq_a, seq_b = _prompt_steps(n_tiles, tiles_per_seq)

    @pl.when(s == 0)
    def _():
        x1_scr[...] = jnp.zeros(x1_scr.shape, F32)
        h2_scr[...] = jnp.zeros(h2_scr.shape, BF16)

    @pl.when(ta % tiles_per_seq == 0)
    def _():
        zbuf[0:CARRY_ROWS, :] = jnp.zeros((CARRY_ROWS, d), F32)

    for src, dst in zip(cast_src, cast_dst):
        dst[...] = src[...].astype(BF16)

    hb_prev = h2_scr[...]
    x = x_ref[...]
    sh1, sc1, g1, sh2, sc2, _ = _split_mod(mod_ref, seq_a)
    hb = _modulated_norm(x, gmix_ref[...], sh1, sc1).astype(BF16)

    n_sh = tt + CARRY_ROWS - SUBLANES
    rb = CONV_ROW_BLOCK
    half = wg_ref.shape[1] // 2
    ffn_parts = {}
    for j in range(d // LANES):
        lanes = slice(j * LANES, (j + 1) * LANES)
        if j in FFN_PART_AFTER_COLUMNS:
            i = FFN_PART_AFTER_COLUMNS.index(j)
            w_ref, cols = (wg_ref, wu_ref)[i // 2], slice((i % 2) * half, (i % 2 + 1) * half)
            ffn_parts[i] = jnp.dot(hb_prev, w_ref[:, cols], preferred_element_type=F32)
        zbuf[CARRY_ROWS:CARRY_ROWS + tt, lanes] = _glu_column(hb, wpw1_ref, bpw1_ref, j)
        for sft in range(1, SUBLANES):
            zsh[sft - 1, :, :] = zbuf[sft:sft + n_sh, lanes]
        for r0 in range(0, tt, rb):
            acc = jnp.zeros((rb, LANES), F32)
            for k in range(CONV_WIDTH):
                q, sft = divmod(CARRY_SKEW + k, SUBLANES)
                rows = slice(r0 + q * SUBLANES, r0 + q * SUBLANES + rb)
                win = zbuf[rows, lanes] if sft == 0 else zsh[sft - 1, rows, :]
                acc = acc + wdw_ref[k:k + 1, lanes] * win
            ybuf[r0:r0 + rb, lanes] = acc + bdw_ref[:, lanes]

    g2_b = _split_mod(mod_ref, seq_b)[5]
    gate = jnp.concatenate([ffn_parts[0], ffn_parts[1]], axis=1)
    up = jnp.concatenate([ffn_parts[2], ffn_parts[3]], axis=1)
    xo_ref[...] = x1_scr[...] + g2_b * _dot(_silu(gate) * up, wd_ref[...])

    x1 = _conv_out(ybuf[...], x, g1, lng_ref[...], lnb_ref[...], wpw2_ref, bpw2_ref[...])
    x1_scr[...] = x1
    h2_scr[...] = _modulated_norm(x1, gffn_ref[...], sh2, sc2).astype(BF16)

    @pl.when(ta % tiles_per_seq == tiles_per_seq - 1)
    def _():
        cs_ref[...] = zbuf[tt + CARRY_SKEW:tt + CARRY_ROWS, :]

    zbuf[0:CARRY_ROWS, :] = zbuf[tt:tt + CARRY_ROWS, :]


def _l0_sample_kernel(x_ref, mod_ref, ctx_ref, nxt_ref, gmix_ref, gffn_ref, wpw1_ref, bpw1_ref, wdw_ref, bdw_ref,
                      lng_ref, lnb_ref, wpw2_ref, bpw2_ref, wg_ref, wu_ref, wd_ref,
                      xo_ref, cso_ref, z_scr, y_scr):
    j = pl.program_id(0)
    last = pl.num_programs(0) - 1

    @pl.when(j == 0)
    def _():
        sh1, sc1 = _split_mod(mod_ref)[:2]
        hb = _modulated_norm(x_ref[...], gmix_ref[...], sh1, sc1).astype(BF16)
        for col in range(D_MODEL // LANES):
            z_scr[:, col * LANES:(col + 1) * LANES] = _glu_column(hb, wpw1_ref, bpw1_ref, col)
        y_scr[...] = jnp.zeros(y_scr.shape, F32) + bdw_ref[...]

    y_scr[...] += wdw_ref[pl.ds(2 * j, 1), :] * ctx_ref[0] + wdw_ref[pl.ds(2 * j + 1, 1), :] * ctx_ref[1]
    cso_ref[0] = ctx_ref[1]

    @pl.when(j < last)
    def _():
        cso_ref[1] = nxt_ref[...]

    @pl.when(j == last)
    def _():
        z = z_scr[...]
        cso_ref[1] = z
        y = y_scr[...] + wdw_ref[CONV_CTX:CONV_WIDTH, :] * z
        _, _, g1, sh2, sc2, g2 = _split_mod(mod_ref)
        x1 = _conv_out(y, x_ref[...], g1, lng_ref[...], lnb_ref[...], wpw2_ref, bpw2_ref[...])
        hb = _modulated_norm(x1, gffn_ref[...], sh2, sc2).astype(BF16)
        xo_ref[...] = x1 + g2 * _swiglu(hb, wg_ref, wu_ref, wd_ref)


def _gmlp_in(x, sh1, sc1, gmix, win_ref, bin, lng, lnb):
    a = jax.nn.gelu(_dot(_modulated_norm(x, gmix, sh1, sc1), win_ref[...]) + bin)
    return a[:, :GMLP_WIDTH], _layernorm(a[:, GMLP_WIDTH:], lng, lnb)


def _l1_prompt_kernel(x_ref, mod_ref, gmix_ref, gffn_ref, gfin_ref, win_ref, bin_ref, lng_ref, lnb_ref,
                      ws_ref, bsf_ref, wout_ref, bout_ref, wg_ref, wu_ref, wd_ref,
                      yo_ref, um_scr, x1_scr, h2_scr, *, n_tiles, tiles_per_seq):
    c, gd = GMLP_CHUNK, GMLP_GROUP_DIM
    tt = x_ref.shape[0]
    s, _, seq_a, seq_b = _prompt_steps(n_tiles, tiles_per_seq)

    @pl.when(s == 0)
    def _():
        x1_scr[...] = jnp.zeros(x1_scr.shape, F32)
        h2_scr[...] = jnp.zeros(h2_scr.shape, BF16)

    hb_prev = h2_scr[...]
    gate = jnp.dot(hb_prev, wg_ref[...], preferred_element_type=F32)

    x = x_ref[...]
    sh1, sc1, g1, sh2, sc2, _ = _split_mod(mod_ref, seq_a)
    u, v = _gmlp_in(x, sh1, sc1, gmix_ref[...], win_ref, bin_ref[...], lng_ref[...], lnb_ref[...])
    vb = v.astype(BF16)

    g2_b = _split_mod(mod_ref, seq_b)[5]
    up = jnp.dot(hb_prev, wu_ref[...], preferred_element_type=F32)
    x2 = x1_scr[...] + g2_b * _dot(_silu(gate) * up, wd_ref[...])
    yo_ref[...] = _rmsnorm(x2, gfin_ref[...])

    row = jax.lax.broadcasted_iota(jnp.int32, (c, c), 0)
    col = jax.lax.broadcasted_iota(jnp.int32, (c, c), 1)
    causal = row >= col
    for g in range(GMLP_GROUPS):
        cols = slice(g * gd, (g + 1) * gd)
        wt = jnp.where(causal, ws_ref[g], 0.0).astype(BF16)
        vg = jnp.concatenate([vb[r0:r0 + c, cols] for r0 in range(0, tt, c)], axis=1)
        m = jnp.dot(wt, vg, preferred_element_type=F32)
        for i, r0 in enumerate(range(0, tt, c)):
            mi = m[:, i * gd:(i + 1) * gd] + bsf_ref[:, cols]
            um_scr[r0:r0 + c, cols] = (u[r0:r0 + c, cols] * mi).astype(BF16)

    x1 = x + g1 * (jnp.dot(um_scr[...], wout_ref[...], preferred_element_type=F32) + bout_ref[...])
    x1_scr[...] = x1
    h2_scr[...] = _modulated_norm(x1, gffn_ref[...], sh2, sc2).astype(BF16)


def _l1_sample_kernel(x_ref, mod_ref, gmix_ref, gffn_ref, gfin_ref, win_ref, bin_ref, lng_ref, lnb_ref,
                      w00_ref, b0_ref, wout_ref, bout_ref, wg_ref, wu_ref, wd_ref, yo_ref, vo_ref):
    x = x_ref[...]
    sh1, sc1, g1, sh2, sc2, g2 = _split_mod(mod_ref)
    u, v = _gmlp_in(x, sh1, sc1, gmix_ref[...], win_ref, bin_ref[...], lng_ref[...], lnb_ref[...])
    vo_ref[...] = v
    um = u * (w00_ref[...] * v + b0_ref[...])
    x1 = x + g1 * (_dot(um, wout_ref[...]) + bout_ref[...])
    hb = _modulated_norm(x1, gffn_ref[...], sh2, sc2).astype(BF16)
    x2 = x1 + g2 * _swiglu(hb, wg_ref, wu_ref, wd_ref)
    yo_ref[...] = _rmsnorm(x2, gfin_ref[...])


class _Layer(NamedTuple):
    stacked: jax.Array
    layer: int


def _resident(c):
    if isinstance(c, _Layer):
        index = (c.layer,) + (0,) * (c.stacked.ndim - 1)
        return pl.BlockSpec((None,) + c.stacked.shape[1:], lambda *_: index, pipeline_mode=pl.Buffered(1))
    index = (0,) * c.ndim
    return pl.BlockSpec(c.shape, lambda *_: index, pipeline_mode=pl.Buffered(1))


def _operand(c):
    return c.stacked if isinstance(c, _Layer) else c


def _row(v):
    return v.reshape(1, -1)


def _params(n_grid):
    return pltpu.CompilerParams(dimension_semantics=("arbitrary",) * n_grid,
                                vmem_limit_bytes=VMEM_LIMIT_BYTES)


def _prompt_tiling(x, tt):
    b, t, d = x.shape
    assert t % tt == 0 and tt % GMLP_CHUNK == 0 and tt >= CARRY_ROWS
    n_tiles = b * t // tt
    mixer_tile = pl.BlockSpec((tt, d), lambda s: (jnp.minimum(s, n_tiles - 1), 0))
    ffn_tile = pl.BlockSpec((tt, d), lambda s: (jnp.maximum(s - 1, 0), 0))
    return x.reshape(b * t, d), mixer_tile, ffn_tile, dict(n_tiles=n_tiles, tiles_per_seq=t // tt)


def _cast_blocks(w, n_steps):
    rows, cols = w.stacked.shape[1:]
    blk = next(r for r in range(BF16_SUBLANES, rows + 1, BF16_SUBLANES) if rows % r == 0 and rows // r <= n_steps)
    last = rows // blk - 1
    src = pl.BlockSpec((None, blk, cols), lambda s: (w.layer, jnp.minimum(s, last), 0))
    dst = pl.BlockSpec((blk, cols), lambda s: (jnp.minimum(s, last), 0))
    return src, dst, jax.ShapeDtypeStruct((rows, cols), BF16)


def _l0_prompt_call(x, mod, consts, cast):
    b, t, d = x.shape
    tt = L0_PROMPT_TILE
    x2d, mixer_tile, ffn_tile, counts = _prompt_tiling(x, tt)
    n_tiles, tiles_per_seq = counts["n_tiles"], counts["tiles_per_seq"]
    assert len(consts) == N_L0_CONSTS
    state = pl.BlockSpec((None, None, CONV_CTX, d),
                         lambda s: (0, jnp.minimum(s, n_tiles - 1) // tiles_per_seq, 0, 0))
    cast_src, cast_dst, cast_shapes = zip(*[_cast_blocks(w, n_tiles + 1) for w in cast])
    xo, cs, *cast_out = pl.pallas_call(
        functools.partial(_l0_prompt_kernel, n_cast=len(cast), **counts),
        grid=(n_tiles + 1,),
        in_specs=[mixer_tile, _resident(mod)] + [_resident(c) for c in consts] + list(cast_src),
        out_specs=[ffn_tile, state] + list(cast_dst),
        out_shape=[jax.ShapeDtypeStruct((b * t, d), F32), jax.ShapeDtypeStruct((1, b, CONV_CTX, d), F32)]
        + list(cast_shapes),
        scratch_shapes=[pltpu.VMEM((CARRY_ROWS + tt, d), F32),
                        pltpu.VMEM((SUBLANES - 1, CARRY_ROWS + tt - SUBLANES, LANES), F32),
                        pltpu.VMEM((tt, d), F32),
                        pltpu.VMEM((tt, d), F32),
                        pltpu.VMEM((tt, d), BF16)],
        compiler_params=_params(1),
        name="l0_prompt",
    )(x2d, _operand(mod), *map(_operand, consts), *map(_operand, cast))
    return xo.reshape(b, t, d), cs, cast_out


def _l0_sample_call(x, mod, ctx_rows, consts):
    n, d = x.shape
    assert CONV_CTX % SAMPLE_TAPS_PER_STEP == 0
    pair = pl.BlockSpec((None, SAMPLE_TAPS_PER_STEP, n, d), lambda j: (0, j, 0, 0))
    nxt = pl.BlockSpec((None, None, n, d),
                       lambda j: (0, jnp.minimum(SAMPLE_TAPS_PER_STEP * (j + 1), CONV_CTX - 1), 0, 0))
    return pl.pallas_call(
        _l0_sample_kernel,
        grid=(CONV_CTX // SAMPLE_TAPS_PER_STEP,),
        in_specs=[_resident(x), _resident(mod), pair, nxt] + [_resident(c) for c in consts],
        out_specs=[pl.BlockSpec((n, d), lambda j: (0, 0)), pair],
        out_shape=[jax.ShapeDtypeStruct((n, d), F32), jax.ShapeDtypeStruct(ctx_rows.shape, F32)],
        scratch_shapes=[pltpu.VMEM((n, d), F32), pltpu.VMEM((n, d), F32)],
        compiler_params=_params(1),
        name="l0_sample",
    )(x, _operand(mod), ctx_rows, ctx_rows, *map(_operand, consts))


def _l1_prompt_call(x, mod, consts):
    b, t, d = x.shape
    tt = L1_PROMPT_TILE
    x2d, mixer_tile, ffn_tile, counts = _prompt_tiling(x, tt)
    y = pl.pallas_call(
        functools.partial(_l1_prompt_kernel, **counts),
        grid=(counts["n_tiles"] + 1,),
        in_specs=[mixer_tile, _resident(mod)] + [_resident(c) for c in consts],
        out_specs=ffn_tile,
        out_shape=jax.ShapeDtypeStruct((b * t, d), F32),
        scratch_shapes=[pltpu.VMEM((tt, GMLP_WIDTH), BF16),
                        pltpu.VMEM((tt, d), F32),
                        pltpu.VMEM((tt, d), BF16)],
        compiler_params=_params(1),
        name="l1_prompt",
    )(x2d, _operand(mod), *map(_operand, consts))
    return y.reshape(b, t, d)


def _l1_sample_call(x, mod, consts):
    n, d = x.shape
    return pl.pallas_call(
        _l1_sample_kernel,
        grid=(1,),
        in_specs=[_resident(x), _resident(mod)] + [_resident(c) for c in consts],
        out_specs=[pl.BlockSpec((n, d), lambda i: (0, 0)), pl.BlockSpec((n, GMLP_WIDTH), lambda i: (0, 0))],
        out_shape=[jax.ShapeDtypeStruct((n, d), F32), jax.ShapeDtypeStruct((n, GMLP_WIDTH), F32)],
        compiler_params=_params(1),
        name="l1_sample",
    )(x, _operand(mod), *map(_operand, consts))


def kernel(x_prompt, x_sample, c_prompt, c_sample, state_conv, w_ada, b_ada, norm_mix_g, norm_ffn_g, final_norm_g, conv_w_pw1, conv_b_pw1, conv_w_dw, conv_b_dw, conv_ln_g, conv_ln_b, conv_w_pw2, conv_b_pw2, gmlp_w_in, gmlp_b_in, gmlp_ln_g, gmlp_ln_b, gmlp_w_s, gmlp_b_s, gmlp_w_out, gmlp_b_out, ffn_w_gate, ffn_w_up, ffn_w_down):
    d = x_prompt.shape[-1]
    ns = x_sample.shape[0]
    assert x_sample.shape[1] == 1 and state_conv.shape[0] == 1 and gmlp_w_in.shape[0] == 1

    mod_p, mod_s = _ada_call(c_prompt, c_sample, w_ada, b_ada)

    l0 = (_row(norm_mix_g[0]), _row(norm_ffn_g[0]), _Layer(conv_w_pw1.astype(BF16), 0), _row(conv_b_pw1[0]),
          _Layer(conv_w_dw, 0), _row(conv_b_dw[0]), _row(conv_ln_g[0]), _row(conv_ln_b[0]),
          _Layer(conv_w_pw2.astype(BF16), 0), _row(conv_b_pw2[0]),
          ffn_w_gate[0].astype(BF16), ffn_w_up[0].astype(BF16), ffn_w_down[0].astype(BF16))

    l1_f32 = (_Layer(gmlp_w_in, 0), _Layer(gmlp_w_out, 0),
              _Layer(ffn_w_gate, 1), _Layer(ffn_w_up, 1), _Layer(ffn_w_down, 1))
    xp, cs_p, (w_in, w_out, w_gate, w_up, w_down) = _l0_prompt_call(x_prompt, _Layer(mod_p, 0), l0, l1_f32)

    l1_head = (_row(norm_mix_g[1]), _row(norm_ffn_g[1]), _row(final_norm_g), w_in,
               _row(gmlp_b_in[0]), _row(gmlp_ln_g[0]), _row(gmlp_ln_b[0]))
    l1_tail = (w_out, _row(gmlp_b_out[0]), w_gate, w_up, w_down)

    bias_full = jnp.repeat(gmlp_b_s[0].T, GMLP_GROUP_DIM, axis=1)
    w00 = _row(jnp.repeat(gmlp_w_s[0][:, 0, 0], GMLP_GROUP_DIM))
    b0 = _row(jnp.repeat(gmlp_b_s[0][:, 0], GMLP_GROUP_DIM))

    y_prompt = _l1_prompt_call(xp, _Layer(mod_p, 1), l1_head + (_Layer(gmlp_w_s, 0), bias_full) + l1_tail)

    xs, cs_rows = _l0_sample_call(x_sample.reshape(ns, d), _Layer(mod_s, 0), jnp.swapaxes(state_conv, 1, 2), l0)
    y_sample, v_s = _l1_sample_call(xs, _Layer(mod_s, 1), l1_head + (w00, b0) + l1_tail)

    return (y_prompt, y_sample.reshape(ns, 1, d), cs_p, jnp.swapaxes(cs_rows, 1, 2),
            v_s.reshape(1, ns, 1, GMLP_WIDTH))
```

```python
import functools
from typing import NamedTuple

import jax
import jax.numpy as jnp
from jax.experimental import pallas as pl
from jax.experimental.pallas import tpu as pltpu

D_MODEL = 1024
CONV_WIDTH = 31
CONV_CTX = CONV_WIDTH - 1
GMLP_CHUNK = 128
GMLP_WIDTH = 2 * D_MODEL
GMLP_GROUPS = 8
GMLP_GROUP_DIM = GMLP_WIDTH // GMLP_GROUPS
EPS = 1e-6

SUBLANES = 8
BF16_SUBLANES = 16
LANES = 128
CARRY_ROWS = 32
CARRY_SKEW = CARRY_ROWS - CONV_CTX
L0_PROMPT_TILE = 256
L1_PROMPT_TILE = 512
CONV_ROW_BLOCK = 64
FFN_PART_AFTER_COLUMNS = (2, 4, 6, 7)
ADA_TILE_N = 1536
SAMPLE_TAPS_PER_STEP = 2
VMEM_LIMIT_BYTES = 60 * 1024 * 1024

F32 = jnp.float32
BF16 = jnp.bfloat16


def _dot(a, b):
    return jnp.dot(a.astype(BF16), b, preferred_element_type=F32)


def _rmsnorm(x, g):
    return x * jax.lax.rsqrt(jnp.mean(x * x, axis=-1, keepdims=True) + EPS) * g


def _layernorm(x, g, b):
    mu = jnp.mean(x, axis=-1, keepdims=True)
    xc = x - mu
    var = jnp.mean(xc * xc, axis=-1, keepdims=True)
    return xc * jax.lax.rsqrt(var + EPS) * g + b


def _silu(x):
    return x * jax.nn.sigmoid(x)


def _modulated_norm(x, g, shift, scale):
    return _rmsnorm(x, g) * (1.0 + scale) + shift


def _swiglu(hb, wg_ref, wu_ref, wd_ref):
    gate = jnp.dot(hb, wg_ref[...], preferred_element_type=F32)
    up = jnp.dot(hb, wu_ref[...], preferred_element_type=F32)
    return _dot(_silu(gate) * up, wd_ref[...])


def _split_mod(mod_ref, row=None):
    d = D_MODEL
    rows = slice(None) if row is None else pl.ds(row, 1)
    return [mod_ref[rows, i * d:(i + 1) * d] for i in range(6)]


def _ada_kernel(cp_ref, cs_ref, w_ref, b_ref, op_ref, os_ref):
    w = w_ref[...].astype(BF16)
    op_ref[...] = _dot(_silu(cp_ref[...]), w) + b_ref[...]
    os_ref[...] = _dot(_silu(cs_ref[...]), w) + b_ref[...]


def _ada_call(c_prompt, c_sample, w_ada, b_ada):
    depth, d, n = w_ada.shape
    col_tile = lambda rows: pl.BlockSpec((None, rows, ADA_TILE_N), lambda i, j: (i, 0, j))
    return pl.pallas_call(
        _ada_kernel,
        grid=(depth, n // ADA_TILE_N),
        in_specs=[
            pl.BlockSpec(c_prompt.shape, lambda i, j: (0, 0)),
            pl.BlockSpec(c_sample.shape, lambda i, j: (0, 0)),
            col_tile(d),
            col_tile(1),
        ],
        out_specs=[col_tile(c_prompt.shape[0]), col_tile(c_sample.shape[0])],
        out_shape=[jax.ShapeDtypeStruct((depth, c_prompt.shape[0], n), F32),
                   jax.ShapeDtypeStruct((depth, c_sample.shape[0], n), F32)],
        compiler_params=pltpu.CompilerParams(dimension_semantics=("arbitrary", "arbitrary")),
        name="adaln_mod",
    )(c_prompt, c_sample, w_ada, b_ada.reshape(depth, 1, n))


def _glu_column(hb, wpw1_ref, bpw1_ref, j):
    value, gate = slice(j * LANES, (j + 1) * LANES), slice(D_MODEL + j * LANES, D_MODEL + (j + 1) * LANES)
    w = jnp.concatenate([wpw1_ref[:, value], wpw1_ref[:, gate]], axis=1)
    b = jnp.concatenate([bpw1_ref[:, value], bpw1_ref[:, gate]], axis=1)
    a = jnp.dot(hb, w, preferred_element_type=F32) + b
    return a[:, :LANES] * jax.nn.sigmoid(a[:, LANES:])


def _conv_out(y, x, g1, lng, lnb, wpw2_ref, bpw2):
    y = _silu(_layernorm(y, lng, lnb))
    return x + g1 * (_dot(y, wpw2_ref[...]) + bpw2)


def _prompt_steps(n_tiles, tiles_per_seq):
    s = pl.program_id(0)
    ta = jnp.minimum(s, n_tiles - 1)
    tb = jnp.maximum(s - 1, 0)
    return s, ta, ta // tiles_per_seq, tb // tiles_per_seq


N_L0_CONSTS = 13


def _l0_prompt_kernel(x_ref, mod_ref, *refs, n_tiles, tiles_per_seq, n_cast):
    (gmix_ref, gffn_ref, wpw1_ref, bpw1_ref, wdw_ref, bdw_ref, lng_ref, lnb_ref, wpw2_ref, bpw2_ref,
     wg_ref, wu_ref, wd_ref) = refs[:N_L0_CONSTS]
    cast_src = refs[N_L0_CONSTS:N_L0_CONSTS + n_cast]
    xo_ref, cs_ref = refs[N_L0_CONSTS + n_cast:N_L0_CONSTS + n_cast + 2]
    cast_dst = refs[N_L0_CONSTS + n_cast + 2:N_L0_CONSTS + 2 * n_cast + 2]
    zbuf, zsh, ybuf, x1_scr, h2_scr = refs[N_L0_CONSTS + 2 * n_cast + 2:]
    d = D_MODEL
    tt = x_ref.shape[0]
    s, ta, seq_a, seq_b = _prompt_steps(n_tiles, tiles_per_seq)

    @pl.when(s == 0)
    def _():
        x1_scr[...] = jnp.zeros(x1_scr.shape, F32)
        h2_scr[...] = jnp.zeros(h2_scr.shape, BF16)

    @pl.when(ta % tiles_per_seq == 0)
    def _():
        zbuf[0:CARRY_ROWS, :] = jnp.zeros((CARRY_ROWS, d), F32)

    for src, dst in zip(cast_src, cast_dst):
        dst[...] = src[...].astype(BF16)

    hb_prev = h2_scr[...]
    x = x_ref[...]
    sh1, sc1, g1, sh2, sc2, _ = _split_mod(mod_ref, seq_a)
    hb = _modulated_norm(x, gmix_ref[...], sh1, sc1).astype(BF16)

    n_sh = tt + CARRY_ROWS - SUBLANES
    rb = CONV_ROW_BLOCK
    half = wg_ref.shape[1] // 2
    ffn_parts = {}
    for j in range(d // LANES):
        lanes = slice(j * LANES, (j + 1) * LANES)
        if j in FFN_PART_AFTER_COLUMNS:
            i = FFN_PART_AFTER_COLUMNS.index(j)
            w_ref, cols = (wg_ref, wu_ref)[i % 2], slice((i // 2) * half, (i // 2 + 1) * half)
            ffn_parts[i] = jnp.dot(hb_prev, w_ref[:, cols], preferred_element_type=F32)
        zbuf[CARRY_ROWS:CARRY_ROWS + tt, lanes] = _glu_column(hb, wpw1_ref, bpw1_ref, j)
        for sft in range(1, SUBLANES):
            zsh[sft - 1, :, :] = zbuf[sft:sft + n_sh, lanes]
        for r0 in range(0, tt, rb):
            acc = jnp.zeros((rb, LANES), F32)
            for k in range(CONV_WIDTH):
                q, sft = divmod(CARRY_SKEW + k, SUBLANES)
                rows = slice(r0 + q * SUBLANES, r0 + q * SUBLANES + rb)
                win = zbuf[rows, lanes] if sft == 0 else zsh[sft - 1, rows, :]
                acc = acc + wdw_ref[k:k + 1, lanes] * win
            ybuf[r0:r0 + rb, lanes] = acc + bdw_ref[:, lanes]

    g2_b = _split_mod(mod_ref, seq_b)[5]
    hidden = jnp.concatenate([(_silu(ffn_parts[2 * i]) * ffn_parts[2 * i + 1]).astype(BF16) for i in range(2)], axis=1)
    xo_ref[...] = x1_scr[...] + g2_b * jnp.dot(hidden, wd_ref[...], preferred_element_type=F32)

    x1 = _conv_out(ybuf[...], x, g1, lng_ref[...], lnb_ref[...], wpw2_ref, bpw2_ref[...])
    x1_scr[...] = x1
    h2_scr[...] = _modulated_norm(x1, gffn_ref[...], sh2, sc2).astype(BF16)

    @pl.when(ta % tiles_per_seq == tiles_per_seq - 1)
    def _():
        cs_ref[...] = zbuf[tt + CARRY_SKEW:tt + CARRY_ROWS, :]

    zbuf[0:CARRY_ROWS, :] = zbuf[tt:tt + CARRY_ROWS, :]


def _l0_sample_kernel(x_ref, mod_ref, ctx_ref, nxt_ref, gmix_ref, gffn_ref, wpw1_ref, bpw1_ref, wdw_ref, bdw_ref,
                      lng_ref, lnb_ref, wpw2_ref, bpw2_ref, wg_ref, wu_ref, wd_ref,
                      xo_ref, cso_ref, z_scr, y_scr):
    j = pl.program_id(0)
    last = pl.num_programs(0) - 1

    @pl.when(j == 0)
    def _():
        sh1, sc1 = _split_mod(mod_ref)[:2]
        hb = _modulated_norm(x_ref[...], gmix_ref[...], sh1, sc1).astype(BF16)
        for col in range(D_MODEL // LANES):
            z_scr[:, col * LANES:(col + 1) * LANES] = _glu_column(hb, wpw1_ref, bpw1_ref, col)
        y_scr[...] = jnp.zeros(y_scr.shape, F32) + bdw_ref[...]

    y_scr[...] += wdw_ref[pl.ds(2 * j, 1), :] * ctx_ref[0] + wdw_ref[pl.ds(2 * j + 1, 1), :] * ctx_ref[1]
    cso_ref[0] = ctx_ref[1]

    @pl.when(j < last)
    def _():
        cso_ref[1] = nxt_ref[...]

    @pl.when(j == last)
    def _():
        z = z_scr[...]
        cso_ref[1] = z
        y = y_scr[...] + wdw_ref[CONV_CTX:CONV_WIDTH, :] * z
        _, _, g1, sh2, sc2, g2 = _split_mod(mod_ref)
        x1 = _conv_out(y, x_ref[...], g1, lng_ref[...], lnb_ref[...], wpw2_ref, bpw2_ref[...])
        hb = _modulated_norm(x1, gffn_ref[...], sh2, sc2).astype(BF16)
        xo_ref[...] = x1 + g2 * _swiglu(hb, wg_ref, wu_ref, wd_ref)


def _gmlp_in(x, sh1, sc1, gmix, win_ref, bin, lng, lnb):
    a = jax.nn.gelu(_dot(_modulated_norm(x, gmix, sh1, sc1), win_ref[...]) + bin)
    return a[:, :GMLP_WIDTH], _layernorm(a[:, GMLP_WIDTH:], lng, lnb)


def _l1_prompt_kernel(x_ref, mod_ref, gmix_ref, gffn_ref, gfin_ref, win_ref, bin_ref, lng_ref, lnb_ref,
                      ws_ref, bsf_ref, wout_ref, bout_ref, wg_ref, wu_ref, wd_ref,
                      yo_ref, um_scr, x1_scr, h2_scr, *, n_tiles, tiles_per_seq):
    c, gd = GMLP_CHUNK, GMLP_GROUP_DIM
    tt = x_ref.shape[0]
    s, _, seq_a, seq_b = _prompt_steps(n_tiles, tiles_per_seq)

    @pl.when(s == 0)
    def _():
        x1_scr[...] = jnp.zeros(x1_scr.shape, F32)
        h2_scr[...] = jnp.zeros(h2_scr.shape, BF16)

    hb_prev = h2_scr[...]
    gate = jnp.dot(hb_prev, wg_ref[...], preferred_element_type=F32)

    x = x_ref[...]
    sh1, sc1, g1, sh2, sc2, _ = _split_mod(mod_ref, seq_a)
    u, v = _gmlp_in(x, sh1, sc1, gmix_ref[...], win_ref, bin_ref[...], lng_ref[...], lnb_ref[...])
    vb = v.astype(BF16)

    g2_b = _split_mod(mod_ref, seq_b)[5]
    up = jnp.dot(hb_prev, wu_ref[...], preferred_element_type=F32)
    x2 = x1_scr[...] + g2_b * _dot(_silu(gate) * up, wd_ref[...])
    yo_ref[...] = _rmsnorm(x2, gfin_ref[...])

    row = jax.lax.broadcasted_iota(jnp.int32, (c, c), 0)
    col = jax.lax.broadcasted_iota(jnp.int32, (c, c), 1)
    causal = row >= col
    for g in range(GMLP_GROUPS):
        cols = slice(g * gd, (g + 1) * gd)
        wt = jnp.where(causal, ws_ref[g], 0.0).astype(BF16)
        vg = jnp.concatenate([vb[r0:r0 + c, cols] for r0 in range(0, tt, c)], axis=1)
        m = jnp.dot(wt, vg, preferred_element_type=F32)
        for i, r0 in enumerate(range(0, tt, c)):
            mi = m[:, i * gd:(i + 1) * gd] + bsf_ref[:, cols]
            um_scr[r0:r0 + c, cols] = (u[r0:r0 + c, cols] * mi).astype(BF16)

    x1 = x + g1 * (jnp.dot(um_scr[...], wout_ref[...], preferred_element_type=F32) + bout_ref[...])
    x1_scr[...] = x1
    h2_scr[...] = _modulated_norm(x1, gffn_ref[...], sh2, sc2).astype(BF16)


def _l1_sample_kernel(x_ref, mod_ref, gmix_ref, gffn_ref, gfin_ref, win_ref, bin_ref, lng_ref, lnb_ref,
                      w00_ref, b0_ref, wout_ref, bout_ref, wg_ref, wu_ref, wd_ref, yo_ref, vo_ref):
    x = x_ref[...]
    sh1, sc1, g1, sh2, sc2, g2 = _split_mod(mod_ref)
    u, v = _gmlp_in(x, sh1, sc1, gmix_ref[...], win_ref, bin_ref[...], lng_ref[...], lnb_ref[...])
    vo_ref[...] = v
    um = u * (w00_ref[...] * v + b0_ref[...])
    x1 = x + g1 * (_dot(um, wout_ref[...]) + bout_ref[...])
    hb = _modulated_norm(x1, gffn_ref[...], sh2, sc2).astype(BF16)
    x2 = x1 + g2 * _swiglu(hb, wg_ref, wu_ref, wd_ref)
    yo_ref[...] = _rmsnorm(x2, gfin_ref[...])


class _Layer(NamedTuple):
    stacked: jax.Array
    layer: int


def _resident(c):
    if isinstance(c, _Layer):
        index = (c.layer,) + (0,) * (c.stacked.ndim - 1)
        return pl.BlockSpec((None,) + c.stacked.shape[1:], lambda *_: index, pipeline_mode=pl.Buffered(1))
    index = (0,) * c.ndim
    return pl.BlockSpec(c.shape, lambda *_: index, pipeline_mode=pl.Buffered(1))


def _operand(c):
    return c.stacked if isinstance(c, _Layer) else c


def _row(v):
    return v.reshape(1, -1)


def _params(n_grid):
    return pltpu.CompilerParams(dimension_semantics=("arbitrary",) * n_grid,
                                vmem_limit_bytes=VMEM_LIMIT_BYTES)


def _prompt_tiling(x, tt):
    b, t, d = x.shape
    assert t % tt == 0 and tt % GMLP_CHUNK == 0 and tt >= CARRY_ROWS
    n_tiles = b * t // tt
    mixer_tile = pl.BlockSpec((tt, d), lambda s: (jnp.minimum(s, n_tiles - 1), 0))
    ffn_tile = pl.BlockSpec((tt, d), lambda s: (jnp.maximum(s - 1, 0), 0))
    return x.reshape(b * t, d), mixer_tile, ffn_tile, dict(n_tiles=n_tiles, tiles_per_seq=t // tt)


def _cast_blocks(w, n_steps):
    rows, cols = w.stacked.shape[1:]
    blk = next(r for r in range(BF16_SUBLANES, rows + 1, BF16_SUBLANES) if rows % r == 0 and rows // r <= n_steps)
    last = rows // blk - 1
    src = pl.BlockSpec((None, blk, cols), lambda s: (w.layer, jnp.minimum(s, last), 0))
    dst = pl.BlockSpec((blk, cols), lambda s: (jnp.minimum(s, last), 0))
    return src, dst, jax.ShapeDtypeStruct((rows, cols), BF16)


def _l0_prompt_call(x, mod, consts, cast):
    b, t, d = x.shape
    tt = L0_PROMPT_TILE
    x2d, mixer_tile, ffn_tile, counts = _prompt_tiling(x, tt)
    n_tiles, tiles_per_seq = counts["n_tiles"], counts["tiles_per_seq"]
    assert len(consts) == N_L0_CONSTS
    state = pl.BlockSpec((None, None, CONV_CTX, d),
                         lambda s: (0, jnp.minimum(s, n_tiles - 1) // tiles_per_seq, 0, 0))
    cast_src, cast_dst, cast_shapes = zip(*[_cast_blocks(w, n_tiles + 1) for w in cast])
    xo, cs, *cast_out = pl.pallas_call(
        functools.partial(_l0_prompt_kernel, n_cast=len(cast), **counts),
        grid=(n_tiles + 1,),
        in_specs=[mixer_tile, _resident(mod)] + [_resident(c) for c in consts] + list(cast_src),
        out_specs=[ffn_tile, state] + list(cast_dst),
        out_shape=[jax.ShapeDtypeStruct((b * t, d), F32), jax.ShapeDtypeStruct((1, b, CONV_CTX, d), F32)]
        + list(cast_shapes),
        scratch_shapes=[pltpu.VMEM((CARRY_ROWS + tt, d), F32),
                        pltpu.VMEM((SUBLANES - 1, CARRY_ROWS + tt - SUBLANES, LANES), F32),
                        pltpu.VMEM((tt, d), F32),
                        pltpu.VMEM((tt, d), F32),
                        pltpu.VMEM((tt, d), BF16)],
        compiler_params=_params(1),
        name="l0_prompt",
    )(x2d, _operand(mod), *map(_operand, consts), *map(_operand, cast))
    return xo.reshape(b, t, d), cs, cast_out


def _l0_sample_call(x, mod, ctx_rows, consts):
    n, d = x.shape
    assert CONV_CTX % SAMPLE_TAPS_PER_STEP == 0
    pair = pl.BlockSpec((None, SAMPLE_TAPS_PER_STEP, n, d), lambda j: (0, j, 0, 0))
    nxt = pl.BlockSpec((None, None, n, d),
                       lambda j: (0, jnp.minimum(SAMPLE_TAPS_PER_STEP * (j + 1), CONV_CTX - 1), 0, 0))
    return pl.pallas_call(
        _l0_sample_kernel,
        grid=(CONV_CTX // SAMPLE_TAPS_PER_STEP,),
        in_specs=[_resident(x), _resident(mod), pair, nxt] + [_resident(c) for c in consts],
        out_specs=[pl.BlockSpec((n, d), lambda j: (0, 0)), pair],
        out_shape=[jax.ShapeDtypeStruct((n, d), F32), jax.ShapeDtypeStruct(ctx_rows.shape, F32)],
        scratch_shapes=[pltpu.VMEM((n, d), F32), pltpu.VMEM((n, d), F32)],
        compiler_params=_params(1),
        name="l0_sample",
    )(x, _operand(mod), ctx_rows, ctx_rows, *map(_operand, consts))


def _l1_prompt_call(x, mod, consts):
    b, t, d = x.shape
    tt = L1_PROMPT_TILE
    x2d, mixer_tile, ffn_tile, counts = _prompt_tiling(x, tt)
    y = pl.pallas_call(
        functools.partial(_l1_prompt_kernel, **counts),
        grid=(counts["n_tiles"] + 1,),
        in_specs=[mixer_tile, _resident(mod)] + [_resident(c) for c in consts],
        out_specs=ffn_tile,
        out_shape=jax.ShapeDtypeStruct((b * t, d), F32),
        scratch_shapes=[pltpu.VMEM((tt, GMLP_WIDTH), BF16),
                        pltpu.VMEM((tt, d), F32),
                        pltpu.VMEM((tt, d), BF16)],
        compiler_params=_params(1),
        name="l1_prompt",
    )(x2d, _operand(mod), *map(_operand, consts))
    return y.reshape(b, t, d)


def _l1_sample_call(x, mod, consts):
    n, d = x.shape
    return pl.pallas_call(
        _l1_sample_kernel,
        grid=(1,),
        in_specs=[_resident(x), _resident(mod)] + [_resident(c) for c in consts],
        out_specs=[pl.BlockSpec((n, d), lambda i: (0, 0)), pl.BlockSpec((n, GMLP_WIDTH), lambda i: (0, 0))],
        out_shape=[jax.ShapeDtypeStruct((n, d), F32), jax.ShapeDtypeStruct((n, GMLP_WIDTH), F32)],
        compiler_params=_params(1),
        name="l1_sample",
    )(x, _operand(mod), *map(_operand, consts))


def kernel(x_prompt, x_sample, c_prompt, c_sample, state_conv, w_ada, b_ada, norm_mix_g, norm_ffn_g, final_norm_g, conv_w_pw1, conv_b_pw1, conv_w_dw, conv_b_dw, conv_ln_g, conv_ln_b, conv_w_pw2, conv_b_pw2, gmlp_w_in, gmlp_b_in, gmlp_ln_g, gmlp_ln_b, gmlp_w_s, gmlp_b_s, gmlp_w_out, gmlp_b_out, ffn_w_gate, ffn_w_up, ffn_w_down):
    d = x_prompt.shape[-1]
    ns = x_sample.shape[0]
    assert x_sample.shape[1] == 1 and state_conv.shape[0] == 1 and gmlp_w_in.shape[0] == 1

    mod_p, mod_s = _ada_call(c_prompt, c_sample, w_ada, b_ada)

    l0 = (_row(norm_mix_g[0]), _row(norm_ffn_g[0]), _Layer(conv_w_pw1.astype(BF16), 0), _row(conv_b_pw1[0]),
          _Layer(conv_w_dw, 0), _row(conv_b_dw[0]), _row(conv_ln_g[0]), _row(conv_ln_b[0]),
          _Layer(conv_w_pw2.astype(BF16), 0), _row(conv_b_pw2[0]),
          ffn_w_gate[0].astype(BF16), ffn_w_up[0].astype(BF16), ffn_w_down[0].astype(BF16))

    l1_f32 = (_Layer(gmlp_w_in, 0), _Layer(gmlp_w_out, 0),
              _Layer(ffn_w_gate, 1), _Layer(ffn_w_up, 1), _Layer(ffn_w_down, 1))
    xp, cs_p, (w_in, w_out, w_gate, w_up, w_down) = _l0_prompt_call(x_prompt, _Layer(mod_p, 0), l0, l1_f32)

    l1_head = (_row(norm_mix_g[1]), _row(norm_ffn_g[1]), _row(final_norm_g), w_in,
               _row(gmlp_b_in[0]), _row(gmlp_ln_g[0]), _row(gmlp_ln_b[0]))
    l1_tail = (w_out, _row(gmlp_b_out[0]), w_gate, w_up, w_down)

    bias_full = jnp.repeat(gmlp_b_s[0].T, GMLP_GROUP_DIM, axis=1)
    w00 = _row(jnp.repeat(gmlp_w_s[0][:, 0, 0], GMLP_GROUP_DIM))
    b0 = _row(jnp.repeat(gmlp_b_s[0][:, 0], GMLP_GROUP_DIM))

    y_prompt = _l1_prompt_call(xp, _Layer(mod_p, 1), l1_head + (_Layer(gmlp_w_s, 0), bias_full) + l1_tail)

    xs, cs_rows = _l0_sample_call(x_sample.reshape(ns, d), _Layer(mod_s, 0), jnp.swapaxes(state_conv, 1, 2), l0)
    y_sample, v_s = _l1_sample_call(xs, _Layer(mod_s, 1), l1_head + (w00, b0) + l1_tail)

    return (y_prompt, y_sample.reshape(ns, 1, d), cs_p, jnp.swapaxes(cs_rows, 1, 2),
            v_s.reshape(1, ns, 1, GMLP_WIDTH))
```

```python
import functools
from typing import NamedTuple

import jax
import jax.numpy as jnp
from jax.experimental import pallas as pl
from jax.experimental.pallas import tpu as pltpu

D_MODEL = 1024
CONV_WIDTH = 31
CONV_CTX = CONV_WIDTH - 1
GMLP_CHUNK = 128
GMLP_WIDTH = 2 * D_MODEL
GMLP_GROUPS = 8
GMLP_GROUP_DIM = GMLP_WIDTH // GMLP_GROUPS
EPS = 1e-6

SUBLANES = 8
BF16_SUBLANES = 16
LANES = 128
CARRY_ROWS = 32
CARRY_SKEW = CARRY_ROWS - CONV_CTX
L0_PROMPT_TILE = 256
L1_PROMPT_TILE = 512
CONV_ROW_BLOCK = 64
ZSH_SLOTS = 4
FFN_PART_AFTER_COLUMNS = (2, 4, 6, 7)
ADA_TILE_N = 1536
SAMPLE_TAPS_PER_STEP = 2
VMEM_LIMIT_BYTES = 60 * 1024 * 1024

F32 = jnp.float32
BF16 = jnp.bfloat16


def _dot(a, b):
    return jnp.dot(a.astype(BF16), b, preferred_element_type=F32)


def _rmsnorm(x, g):
    return x * jax.lax.rsqrt(jnp.mean(x * x, axis=-1, keepdims=True) + EPS) * g


def _layernorm(x, g, b):
    mu = jnp.mean(x, axis=-1, keepdims=True)
    xc = x - mu
    var = jnp.mean(xc * xc, axis=-1, keepdims=True)
    return xc * jax.lax.rsqrt(var + EPS) * g + b


def _silu(x):
    return x * jax.nn.sigmoid(x)


def _modulated_norm(x, g, shift, scale):
    return _rmsnorm(x, g) * (1.0 + scale) + shift


def _swiglu(hb, wg_ref, wu_ref, wd_ref):
    gate = jnp.dot(hb, wg_ref[...], preferred_element_type=F32)
    up = jnp.dot(hb, wu_ref[...], preferred_element_type=F32)
    return _dot(_silu(gate) * up, wd_ref[...])


def _split_mod(mod_ref, row=None):
    d = D_MODEL
    rows = slice(None) if row is None else pl.ds(row, 1)
    return [mod_ref[rows, i * d:(i + 1) * d] for i in range(6)]


def _ada_kernel(cp_ref, cs_ref, w_ref, b_ref, op_ref, os_ref):
    w = w_ref[...].astype(BF16)
    op_ref[...] = _dot(_silu(cp_ref[...]), w) + b_ref[...]
    os_ref[...] = _dot(_silu(cs_ref[...]), w) + b_ref[...]


def _ada_call(c_prompt, c_sample, w_ada, b_ada):
    depth, d, n = w_ada.shape
    col_tile = lambda rows: pl.BlockSpec((None, rows, ADA_TILE_N), lambda i, j: (i, 0, j))
    return pl.pallas_call(
        _ada_kernel,
        grid=(depth, n // ADA_TILE_N),
        in_specs=[
            pl.BlockSpec(c_prompt.shape, lambda i, j: (0, 0)),
            pl.BlockSpec(c_sample.shape, lambda i, j: (0, 0)),
            col_tile(d),
            col_tile(1),
        ],
        out_specs=[col_tile(c_prompt.shape[0]), col_tile(c_sample.shape[0])],
        out_shape=[jax.ShapeDtypeStruct((depth, c_prompt.shape[0], n), F32),
                   jax.ShapeDtypeStruct((depth, c_sample.shape[0], n), F32)],
        compiler_params=pltpu.CompilerParams(dimension_semantics=("arbitrary", "arbitrary")),
        name="adaln_mod",
    )(c_prompt, c_sample, w_ada, b_ada.reshape(depth, 1, n))


def _glu_column(hb, wpw1_ref, bpw1_ref, j):
    value, gate = slice(j * LANES, (j + 1) * LANES), slice(D_MODEL + j * LANES, D_MODEL + (j + 1) * LANES)
    w = jnp.concatenate([wpw1_ref[:, value], wpw1_ref[:, gate]], axis=1)
    b = jnp.concatenate([bpw1_ref[:, value], bpw1_ref[:, gate]], axis=1)
    a = jnp.dot(hb, w, preferred_element_type=F32) + b
    return a[:, :LANES] * jax.nn.sigmoid(a[:, LANES:])


def _conv_out(y, x, g1, lng, lnb, wpw2_ref, bpw2):
    y = _silu(_layernorm(y, lng, lnb))
    return x + g1 * (_dot(y, wpw2_ref[...]) + bpw2)


def _prompt_steps(n_tiles, tiles_per_seq):
    s = pl.program_id(0)
    ta = jnp.minimum(s, n_tiles - 1)
    tb = jnp.maximum(s - 1, 0)
    return s, ta, ta // tiles_per_seq, tb // tiles_per_seq


N_L0_CONSTS = 13


def _l0_prompt_kernel(x_ref, mod_ref, *refs, n_tiles, tiles_per_seq, n_cast):
    (gmix_ref, gffn_ref, wpw1_ref, bpw1_ref, wdw_ref, bdw_ref, lng_ref, lnb_ref, wpw2_ref, bpw2_ref,
     wg_ref, wu_ref, wd_ref) = refs[:N_L0_CONSTS]
    cast_src = refs[N_L0_CONSTS:N_L0_CONSTS + n_cast]
    xo_ref, cs_ref = refs[N_L0_CONSTS + n_cast:N_L0_CONSTS + n_cast + 2]
    cast_dst = refs[N_L0_CONSTS + n_cast + 2:N_L0_CONSTS + 2 * n_cast + 2]
    zbuf, zsh, ybuf, x1_scr, h2_scr = refs[N_L0_CONSTS + 2 * n_cast + 2:]
    d = D_MODEL
    tt = x_ref.shape[0]
    s, ta, seq_a, seq_b = _prompt_steps(n_tiles, tiles_per_seq)

    @pl.when(s == 0)
    def _():
        x1_scr[...] = jnp.zeros(x1_scr.shape, F32)
        h2_scr[...] = jnp.zeros(h2_scr.shape, BF16)

    @pl.when(ta % tiles_per_seq == 0)
    def _():
        zbuf[0:CARRY_ROWS, :] = jnp.zeros((CARRY_ROWS, d), F32)

    for src, dst in zip(cast_src, cast_dst):
        dst[...] = src[...].astype(BF16)

    hb_prev = h2_scr[...]
    x = x_ref[...]
    sh1, sc1, g1, sh2, sc2, _ = _split_mod(mod_ref, seq_a)
    hb = _modulated_norm(x, gmix_ref[...], sh1, sc1).astype(BF16)

    n_sh = tt + CARRY_ROWS - SUBLANES
    rb = CONV_ROW_BLOCK
    half = wg_ref.shape[1] // 2
    ffn_parts = {}
    for j in range(d // LANES):
        lanes = slice(j * LANES, (j + 1) * LANES)
        if j in FFN_PART_AFTER_COLUMNS:
            i = FFN_PART_AFTER_COLUMNS.index(j)
            w_ref, cols = (wg_ref, wu_ref)[i % 2], slice((i // 2) * half, (i // 2 + 1) * half)
            ffn_parts[i] = jnp.dot(hb_prev, w_ref[:, cols], preferred_element_type=F32)
        zbuf[CARRY_ROWS:CARRY_ROWS + tt, lanes] = _glu_column(hb, wpw1_ref, bpw1_ref, j)
        for sft in range(1, SUBLANES):
            zsh[j % ZSH_SLOTS, sft - 1, :, :] = zbuf[sft:sft + n_sh, lanes]
        for r0 in range(0, tt, rb):
            acc = jnp.zeros((rb, LANES), F32)
            for k in range(CONV_WIDTH):
                q, sft = divmod(CARRY_SKEW + k, SUBLANES)
                rows = slice(r0 + q * SUBLANES, r0 + q * SUBLANES + rb)
                win = zbuf[rows, lanes] if sft == 0 else zsh[j % ZSH_SLOTS, sft - 1, rows, :]
                acc = acc + wdw_ref[k:k + 1, lanes] * win
            ybuf[r0:r0 + rb, lanes] = acc + bdw_ref[:, lanes]

    g2_b = _split_mod(mod_ref, seq_b)[5]
    hidden = jnp.concatenate([(_silu(ffn_parts[2 * i]) * ffn_parts[2 * i + 1]).astype(BF16) for i in range(2)], axis=1)
    xo_ref[...] = x1_scr[...] + g2_b * jnp.dot(hidden, wd_ref[...], preferred_element_type=F32)

    x1 = _conv_out(ybuf[...], x, g1, lng_ref[...], lnb_ref[...], wpw2_ref, bpw2_ref[...])
    x1_scr[...] = x1
    h2_scr[...] = _modulated_norm(x1, gffn_ref[...], sh2, sc2).astype(BF16)

    @pl.when(ta % tiles_per_seq == tiles_per_seq - 1)
    def _():
        cs_ref[...] = zbuf[tt + CARRY_SKEW:tt + CARRY_ROWS, :]

    zbuf[0:CARRY_ROWS, :] = zbuf[tt:tt + CARRY_ROWS, :]


def _l0_sample_kernel(x_ref, mod_ref, ctx_ref, nxt_ref, gmix_ref, gffn_ref, wpw1_ref, bpw1_ref, wdw_ref, bdw_ref,
                      lng_ref, lnb_ref, wpw2_ref, bpw2_ref, wg_ref, wu_ref, wd_ref,
                      xo_ref, cso_ref, z_scr, y_scr):
    j = pl.program_id(0)
    last = pl.num_programs(0) - 1

    @pl.when(j == 0)
    def _():
        sh1, sc1 = _split_mod(mod_ref)[:2]
        hb = _modulated_norm(x_ref[...], gmix_ref[...], sh1, sc1).astype(BF16)
        for col in range(D_MODEL // LANES):
            z_scr[:, col * LANES:(col + 1) * LANES] = _glu_column(hb, wpw1_ref, bpw1_ref, col)
        y_scr[...] = jnp.zeros(y_scr.shape, F32) + bdw_ref[...]

    y_scr[...] += wdw_ref[pl.ds(2 * j, 1), :] * ctx_ref[0] + wdw_ref[pl.ds(2 * j + 1, 1), :] * ctx_ref[1]
    cso_ref[0] = ctx_ref[1]

    @pl.when(j < last)
    def _():
        cso_ref[1] = nxt_ref[...]

    @pl.when(j == last)
    def _():
        z = z_scr[...]
        cso_ref[1] = z
        y = y_scr[...] + wdw_ref[CONV_CTX:CONV_WIDTH, :] * z
        _, _, g1, sh2, sc2, g2 = _split_mod(mod_ref)
        x1 = _conv_out(y, x_ref[...], g1, lng_ref[...], lnb_ref[...], wpw2_ref, bpw2_ref[...])
        hb = _modulated_norm(x1, gffn_ref[...], sh2, sc2).astype(BF16)
        xo_ref[...] = x1 + g2 * _swiglu(hb, wg_ref, wu_ref, wd_ref)


def _gmlp_in(x, sh1, sc1, gmix, win_ref, bin, lng, lnb):
    a = jax.nn.gelu(_dot(_modulated_norm(x, gmix, sh1, sc1), win_ref[...]) + bin)
    return a[:, :GMLP_WIDTH], _layernorm(a[:, GMLP_WIDTH:], lng, lnb)


def _l1_prompt_kernel(x_ref, mod_ref, gmix_ref, gffn_ref, gfin_ref, win_ref, bin_ref, lng_ref, lnb_ref,
                      ws_ref, bsf_ref, wout_ref, bout_ref, wg_ref, wu_ref, wd_ref,
                      yo_ref, um_scr, x1_scr, h2_scr, *, n_tiles, tiles_per_seq):
    c, gd = GMLP_CHUNK, GMLP_GROUP_DIM
    tt = x_ref.shape[0]
    s, _, seq_a, seq_b = _prompt_steps(n_tiles, tiles_per_seq)

    @pl.when(s == 0)
    def _():
        x1_scr[...] = jnp.zeros(x1_scr.shape, F32)
        h2_scr[...] = jnp.zeros(h2_scr.shape, BF16)

    hb_prev = h2_scr[...]
    gate = jnp.dot(hb_prev, wg_ref[...], preferred_element_type=F32)

    x = x_ref[...]
    sh1, sc1, g1, sh2, sc2, _ = _split_mod(mod_ref, seq_a)
    u, v = _gmlp_in(x, sh1, sc1, gmix_ref[...], win_ref, bin_ref[...], lng_ref[...], lnb_ref[...])
    vb = v.astype(BF16)

    g2_b = _split_mod(mod_ref, seq_b)[5]
    up = jnp.dot(hb_prev, wu_ref[...], preferred_element_type=F32)
    x2 = x1_scr[...] + g2_b * _dot(_silu(gate) * up, wd_ref[...])
    yo_ref[...] = _rmsnorm(x2, gfin_ref[...])

    row = jax.lax.broadcasted_iota(jnp.int32, (c, c), 0)
    col = jax.lax.broadcasted_iota(jnp.int32, (c, c), 1)
    causal = row >= col
    for g in range(GMLP_GROUPS):
        cols = slice(g * gd, (g + 1) * gd)
        wt = jnp.where(causal, ws_ref[g], 0.0).astype(BF16)
        vg = jnp.concatenate([vb[r0:r0 + c, cols] for r0 in range(0, tt, c)], axis=1)
        m = jnp.dot(wt, vg, preferred_element_type=F32)
        for i, r0 in enumerate(range(0, tt, c)):
            mi = m[:, i * gd:(i + 1) * gd] + bsf_ref[:, cols]
            um_scr[r0:r0 + c, cols] = (u[r0:r0 + c, cols] * mi).astype(BF16)

    x1 = x + g1 * (jnp.dot(um_scr[...], wout_ref[...], preferred_element_type=F32) + bout_ref[...])
    x1_scr[...] = x1
    h2_scr[...] = _modulated_norm(x1, gffn_ref[...], sh2, sc2).astype(BF16)


def _l1_sample_kernel(x_ref, mod_ref, gmix_ref, gffn_ref, gfin_ref, win_ref, bin_ref, lng_ref, lnb_ref,
                      w00_ref, b0_ref, wout_ref, bout_ref, wg_ref, wu_ref, wd_ref, yo_ref, vo_ref):
    x = x_ref[...]
    sh1, sc1, g1, sh2, sc2, g2 = _split_mod(mod_ref)
    u, v = _gmlp_in(x, sh1, sc1, gmix_ref[...], win_ref, bin_ref[...], lng_ref[...], lnb_ref[...])
    vo_ref[...] = v
    um = u * (w00_ref[...] * v + b0_ref[...])
    x1 = x + g1 * (_dot(um, wout_ref[...]) + bout_ref[...])
    hb = _modulated_norm(x1, gffn_ref[...], sh2, sc2).astype(BF16)
    x2 = x1 + g2 * _swiglu(hb, wg_ref, wu_ref, wd_ref)
    yo_ref[...] = _rmsnorm(x2, gfin_ref[...])


class _Layer(NamedTuple):
    stacked: jax.Array
    layer: int


def _resident(c):
    if isinstance(c, _Layer):
        index = (c.layer,) + (0,) * (c.stacked.ndim - 1)
        return pl.BlockSpec((None,) + c.stacked.shape[1:], lambda *_: index, pipeline_mode=pl.Buffered(1))
    index = (0,) * c.ndim
    return pl.BlockSpec(c.shape, lambda *_: index, pipeline_mode=pl.Buffered(1))


def _operand(c):
    return c.stacked if isinstance(c, _Layer) else c


def _row(v):
    return v.reshape(1, -1)


def _params(n_grid):
    return pltpu.CompilerParams(dimension_semantics=("arbitrary",) * n_grid,
                                vmem_limit_bytes=VMEM_LIMIT_BYTES)


def _prompt_tiling(x, tt):
    b, t, d = x.shape
    assert t % tt == 0 and tt % GMLP_CHUNK == 0 and tt >= CARRY_ROWS
    n_tiles = b * t // tt
    mixer_tile = pl.BlockSpec((tt, d), lambda s: (jnp.minimum(s, n_tiles - 1), 0))
    ffn_tile = pl.BlockSpec((tt, d), lambda s: (jnp.maximum(s - 1, 0), 0))
    return x.reshape(b * t, d), mixer_tile, ffn_tile, dict(n_tiles=n_tiles, tiles_per_seq=t // tt)


def _cast_blocks(w, n_steps):
    rows, cols = w.stacked.shape[1:]
    blk = next(r for r in range(BF16_SUBLANES, rows + 1, BF16_SUBLANES) if rows % r == 0 and rows // r <= n_steps)
    last = rows // blk - 1
    src = pl.BlockSpec((None, blk, cols), lambda s: (w.layer, jnp.minimum(s, last), 0))
    dst = pl.BlockSpec((blk, cols), lambda s: (jnp.minimum(s, last), 0))
    return src, dst, jax.ShapeDtypeStruct((rows, cols), BF16)


def _l0_prompt_call(x, mod, consts, cast):
    b, t, d = x.shape
    tt = L0_PROMPT_TILE
    x2d, mixer_tile, ffn_tile, counts = _prompt_tiling(x, tt)
    n_tiles, tiles_per_seq = counts["n_tiles"], counts["tiles_per_seq"]
    assert len(consts) == N_L0_CONSTS
    state = pl.BlockSpec((None, None, CONV_CTX, d),
                         lambda s: (0, jnp.minimum(s, n_tiles - 1) // tiles_per_seq, 0, 0))
    cast_src, cast_dst, cast_shapes = zip(*[_cast_blocks(w, n_tiles + 1) for w in cast])
    xo, cs, *cast_out = pl.pallas_call(
        functools.partial(_l0_prompt_kernel, n_cast=len(cast), **counts),
        grid=(n_tiles + 1,),
        in_specs=[mixer_tile, _resident(mod)] + [_resident(c) for c in consts] + list(cast_src),
        out_specs=[ffn_tile, state] + list(cast_dst),
        out_shape=[jax.ShapeDtypeStruct((b * t, d), F32), jax.ShapeDtypeStruct((1, b, CONV_CTX, d), F32)]
        + list(cast_shapes),
        scratch_shapes=[pltpu.VMEM((CARRY_ROWS + tt, d), F32),
                        pltpu.VMEM((ZSH_SLOTS, SUBLANES - 1, CARRY_ROWS + tt - SUBLANES, LANES), F32),
                        pltpu.VMEM((tt, d), F32),
                        pltpu.VMEM((tt, d), F32),
                        pltpu.VMEM((tt, d), BF16)],
        compiler_params=_params(1),
        name="l0_prompt",
    )(x2d, _operand(mod), *map(_operand, consts), *map(_operand, cast))
    return xo.reshape(b, t, d), cs, cast_out


def _l0_sample_call(x, mod, ctx_rows, consts):
    n, d = x.shape
    assert CONV_CTX % SAMPLE_TAPS_PER_STEP == 0
    pair = pl.BlockSpec((None, SAMPLE_TAPS_PER_STEP, n, d), lambda j: (0, j, 0, 0))
    nxt = pl.BlockSpec((None, None, n, d),
                       lambda j: (0, jnp.minimum(SAMPLE_TAPS_PER_STEP * (j + 1), CONV_CTX - 1), 0, 0))
    return pl.pallas_call(
        _l0_sample_kernel,
        grid=(CONV_CTX // SAMPLE_TAPS_PER_STEP,),
        in_specs=[_resident(x), _resident(mod), pair, nxt] + [_resident(c) for c in consts],
        out_specs=[pl.BlockSpec((n, d), lambda j: (0, 0)), pair],
        out_shape=[jax.ShapeDtypeStruct((n, d), F32), jax.ShapeDtypeStruct(ctx_rows.shape, F32)],
        scratch_shapes=[pltpu.VMEM((n, d), F32), pltpu.VMEM((n, d), F32)],
        compiler_params=_params(1),
        name="l0_sample",
    )(x, _operand(mod), ctx_rows, ctx_rows, *map(_operand, consts))


def _l1_prompt_call(x, mod, consts):
    b, t, d = x.shape
    tt = L1_PROMPT_TILE
    x2d, mixer_tile, ffn_tile, counts = _prompt_tiling(x, tt)
    y = pl.pallas_call(
        functools.partial(_l1_prompt_kernel, **counts),
        grid=(counts["n_tiles"] + 1,),
        in_specs=[mixer_tile, _resident(mod)] + [_resident(c) for c in consts],
        out_specs=ffn_tile,
        out_shape=jax.ShapeDtypeStruct((b * t, d), F32),
        scratch_shapes=[pltpu.VMEM((tt, GMLP_WIDTH), BF16),
                        pltpu.VMEM((tt, d), F32),
                        pltpu.VMEM((tt, d), BF16)],
        compiler_params=_params(1),
        name="l1_prompt",
    )(x2d, _operand(mod), *map(_operand, consts))
    return y.reshape(b, t, d)


def _l1_sample_call(x, mod, consts):
    n, d = x.shape
    return pl.pallas_call(
        _l1_sample_kernel,
        grid=(1,),
        in_specs=[_resident(x), _resident(mod)] + [_resident(c) for c in consts],
        out_specs=[pl.BlockSpec((n, d), lambda i: (0, 0)), pl.BlockSpec((n, GMLP_WIDTH), lambda i: (0, 0))],
        out_shape=[jax.ShapeDtypeStruct((n, d), F32), jax.ShapeDtypeStruct((n, GMLP_WIDTH), F32)],
        compiler_params=_params(1),
        name="l1_sample",
    )(x, _operand(mod), *map(_operand, consts))


def kernel(x_prompt, x_sample, c_prompt, c_sample, state_conv, w_ada, b_ada, norm_mix_g, norm_ffn_g, final_norm_g, conv_w_pw1, conv_b_pw1, conv_w_dw, conv_b_dw, conv_ln_g, conv_ln_b, conv_w_pw2, conv_b_pw2, gmlp_w_in, gmlp_b_in, gmlp_ln_g, gmlp_ln_b, gmlp_w_s, gmlp_b_s, gmlp_w_out, gmlp_b_out, ffn_w_gate, ffn_w_up, ffn_w_down):
    d = x_prompt.shape[-1]
    ns = x_sample.shape[0]
    assert x_sample.shape[1] == 1 and state_conv.shape[0] == 1 and gmlp_w_in.shape[0] == 1

    mod_p, mod_s = _ada_call(c_prompt, c_sample, w_ada, b_ada)

    l0 = (_row(norm_mix_g[0]), _row(norm_ffn_g[0]), _Layer(conv_w_pw1.astype(BF16), 0), _row(conv_b_pw1[0]),
          _Layer(conv_w_dw, 0), _row(conv_b_dw[0]), _row(conv_ln_g[0]), _row(conv_ln_b[0]),
          _Layer(conv_w_pw2.astype(BF16), 0), _row(conv_b_pw2[0]),
          ffn_w_gate[0].astype(BF16), ffn_w_up[0].astype(BF16), ffn_w_down[0].astype(BF16))

    l1_f32 = (_Layer(gmlp_w_in, 0), _Layer(gmlp_w_out, 0),
              _Layer(ffn_w_gate, 1), _Layer(ffn_w_up, 1), _Layer(ffn_w_down, 1))
    xp, cs_p, (w_in, w_out, w_gate, w_up, w_down) = _l0_prompt_call(x_prompt, _Layer(mod_p, 0), l0, l1_f32)

    l1_head = (_row(norm_mix_g[1]), _row(norm_ffn_g[1]), _row(final_norm_g), w_in,
               _row(gmlp_b_in[0]), _row(gmlp_ln_g[0]), _row(gmlp_ln_b[0]))
    l1_tail = (w_out, _row(gmlp_b_out[0]), w_gate, w_up, w_down)

    bias_full = jnp.repeat(gmlp_b_s[0].T, GMLP_GROUP_DIM, axis=1)
    w00 = _row(jnp.repeat(gmlp_w_s[0][:, 0, 0], GMLP_GROUP_DIM))
    b0 = _row(jnp.repeat(gmlp_b_s[0][:, 0], GMLP_GROUP_DIM))

    y_prompt = _l1_prompt_call(xp, _Layer(mod_p, 1), l1_head + (_Layer(gmlp_w_s, 0), bias_full) + l1_tail)

    xs, cs_rows = _l0_sample_call(x_sample.reshape(ns, d), _Layer(mod_s, 0), jnp.swapaxes(state_conv, 1, 2), l0)
    y_sample, v_s = _l1_sample_call(xs, _Layer(mod_s, 1), l1_head + (w00, b0) + l1_tail)

    return (y_prompt, y_sample.reshape(ns, 1, d), cs_p, jnp.swapaxes(cs_rows, 1, 2),
            v_s.reshape(1, ns, 1, GMLP_WIDTH))
```

```python
import functools
from typing import NamedTuple

import jax
import jax.numpy as jnp
from jax.experimental import pallas as pl
from jax.experimental.pallas import tpu as pltpu

D_MODEL = 1024
CONV_WIDTH = 31
CONV_CTX = CONV_WIDTH - 1
GMLP_CHUNK = 128
GMLP_WIDTH = 2 * D_MODEL
GMLP_GROUPS = 8
GMLP_GROUP_DIM = GMLP_WIDTH // GMLP_GROUPS
EPS = 1e-6

SUBLANES = 8
BF16_SUBLANES = 16
LANES = 128
CARRY_ROWS = 32
CARRY_SKEW = CARRY_ROWS - CONV_CTX
L0_PROMPT_TILE = 256
L1_PROMPT_TILE = 512
CONV_ROW_BLOCK = 64
ZSH_SLOTS = 4
FFN_PART_AFTER_COLUMNS = (2, 4, 6, 7)
ADA_TILE_N = 1536
SAMPLE_TAPS_PER_STEP = 2
VMEM_LIMIT_BYTES = 60 * 1024 * 1024

F32 = jnp.float32
BF16 = jnp.bfloat16


def _dot(a, b):
    return jnp.dot(a.astype(BF16), b, preferred_element_type=F32)


def _rmsnorm(x, g):
    return x * jax.lax.rsqrt(jnp.mean(x * x, axis=-1, keepdims=True) + EPS) * g


def _layernorm(x, g, b):
    mu = jnp.mean(x, axis=-1, keepdims=True)
    xc = x - mu
    var = jnp.mean(xc * xc, axis=-1, keepdims=True)
    return xc * jax.lax.rsqrt(var + EPS) * g + b


def _silu(x):
    return x * jax.nn.sigmoid(x)


def _modulated_norm(x, g, shift, scale):
    return _rmsnorm(x, g) * (1.0 + scale) + shift


def _swiglu(hb, wg_ref, wu_ref, wd_ref):
    gate = jnp.dot(hb, wg_ref[...], preferred_element_type=F32)
    up = jnp.dot(hb, wu_ref[...], preferred_element_type=F32)
    return _dot(_silu(gate) * up, wd_ref[...])


def _split_mod(mod_ref, row=None):
    d = D_MODEL
    rows = slice(None) if row is None else pl.ds(row, 1)
    return [mod_ref[rows, i * d:(i + 1) * d] for i in range(6)]


def _ada_kernel(cp_ref, cs_ref, w_ref, b_ref, op_ref, os_ref):
    w = w_ref[...].astype(BF16)
    op_ref[...] = _dot(_silu(cp_ref[...]), w) + b_ref[...]
    os_ref[...] = _dot(_silu(cs_ref[...]), w) + b_ref[...]


def _ada_call(c_prompt, c_sample, w_ada, b_ada):
    depth, d, n = w_ada.shape
    col_tile = lambda rows: pl.BlockSpec((None, rows, ADA_TILE_N), lambda i, j: (i, 0, j))
    return pl.pallas_call(
        _ada_kernel,
        grid=(depth, n // ADA_TILE_N),
        in_specs=[
            pl.BlockSpec(c_prompt.shape, lambda i, j: (0, 0)),
            pl.BlockSpec(c_sample.shape, lambda i, j: (0, 0)),
            col_tile(d),
            col_tile(1),
        ],
        out_specs=[col_tile(c_prompt.shape[0]), col_tile(c_sample.shape[0])],
        out_shape=[jax.ShapeDtypeStruct((depth, c_prompt.shape[0], n), F32),
                   jax.ShapeDtypeStruct((depth, c_sample.shape[0], n), F32)],
        compiler_params=pltpu.CompilerParams(dimension_semantics=("arbitrary", "arbitrary")),
        name="adaln_mod",
    )(c_prompt, c_sample, w_ada, b_ada.reshape(depth, 1, n))


def _glu_column(hb, wpw1_ref, bpw1_ref, j):
    value, gate = slice(j * LANES, (j + 1) * LANES), slice(D_MODEL + j * LANES, D_MODEL + (j + 1) * LANES)
    w = jnp.concatenate([wpw1_ref[:, value], wpw1_ref[:, gate]], axis=1)
    b = jnp.concatenate([bpw1_ref[:, value], bpw1_ref[:, gate]], axis=1)
    a = jnp.dot(hb, w, preferred_element_type=F32) + b
    return a[:, :LANES] * jax.nn.sigmoid(a[:, LANES:])


def _conv_out(y, x, g1, lng, lnb, wpw2_ref, bpw2):
    y = _silu(_layernorm(y, lng, lnb))
    return x + g1 * (_dot(y, wpw2_ref[...]) + bpw2)


def _prompt_steps(n_tiles, tiles_per_seq):
    s = pl.program_id(0)
    ta = jnp.minimum(s, n_tiles - 1)
    tb = jnp.maximum(s - 1, 0)
    return s, ta, ta // tiles_per_seq, tb // tiles_per_seq


N_L0_CONSTS = 13


def _l0_prompt_kernel(x_ref, mod_ref, *refs, n_tiles, tiles_per_seq, n_cast):
    (gmix_ref, gffn_ref, wpw1_ref, bpw1_ref, wdw_ref, bdw_ref, lng_ref, lnb_ref, wpw2_ref, bpw2_ref,
     wg_ref, wu_ref, wd_ref) = refs[:N_L0_CONSTS]
    cast_src = refs[N_L0_CONSTS:N_L0_CONSTS + n_cast]
    xo_ref, cs_ref = refs[N_L0_CONSTS + n_cast:N_L0_CONSTS + n_cast + 2]
    cast_dst = refs[N_L0_CONSTS + n_cast + 2:N_L0_CONSTS + 2 * n_cast + 2]
    zbuf, zsh, ybuf, x1_scr, h2_scr = refs[N_L0_CONSTS + 2 * n_cast + 2:]
    d = D_MODEL
    tt = x_ref.shape[0]
    s, ta, seq_a, seq_b = _prompt_steps(n_tiles, tiles_per_seq)

    @pl.when(s == 0)
    def _():
        x1_scr[...] = jnp.zeros(x1_scr.shape, F32)
        h2_scr[...] = jnp.zeros(h2_scr.shape, BF16)

    @pl.when(ta % tiles_per_seq == 0)
    def _():
        zbuf[:, 0:CARRY_ROWS, :] = jnp.zeros((d // LANES, CARRY_ROWS, LANES), F32)

    for src, dst in zip(cast_src, cast_dst):
        dst[...] = src[...].astype(BF16)

    hb_prev = h2_scr[...]
    x = x_ref[...]
    sh1, sc1, g1, sh2, sc2, _ = _split_mod(mod_ref, seq_a)
    hb = _modulated_norm(x, gmix_ref[...], sh1, sc1).astype(BF16)

    n_sh = tt + CARRY_ROWS - SUBLANES
    rb = CONV_ROW_BLOCK
    half = wg_ref.shape[1] // 2
    ffn_parts = {}
    for j in range(d // LANES):
        lanes = slice(j * LANES, (j + 1) * LANES)
        if j in FFN_PART_AFTER_COLUMNS:
            i = FFN_PART_AFTER_COLUMNS.index(j)
            w_ref, cols = (wg_ref, wu_ref)[i % 2], slice((i // 2) * half, (i // 2 + 1) * half)
            ffn_parts[i] = jnp.dot(hb_prev, w_ref[:, cols], preferred_element_type=F32)
        zbuf[j, CARRY_ROWS:CARRY_ROWS + tt, :] = _glu_column(hb, wpw1_ref, bpw1_ref, j)
        for sft in range(1, SUBLANES):
            zsh[j % ZSH_SLOTS, sft - 1, :, :] = zbuf[j, sft:sft + n_sh, :]
        for r0 in range(0, tt, rb):
            acc = jnp.zeros((rb, LANES), F32)
            for k in range(CONV_WIDTH):
                q, sft = divmod(CARRY_SKEW + k, SUBLANES)
                rows = slice(r0 + q * SUBLANES, r0 + q * SUBLANES + rb)
                win = zbuf[j, rows, :] if sft == 0 else zsh[j % ZSH_SLOTS, sft - 1, rows, :]
                acc = acc + wdw_ref[k:k + 1, lanes] * win
            ybuf[r0:r0 + rb, lanes] = acc + bdw_ref[:, lanes]

    g2_b = _split_mod(mod_ref, seq_b)[5]
    hidden = jnp.concatenate([(_silu(ffn_parts[2 * i]) * ffn_parts[2 * i + 1]).astype(BF16) for i in range(2)], axis=1)
    xo_ref[...] = x1_scr[...] + g2_b * jnp.dot(hidden, wd_ref[...], preferred_element_type=F32)

    x1 = _conv_out(ybuf[...], x, g1, lng_ref[...], lnb_ref[...], wpw2_ref, bpw2_ref[...])
    x1_scr[...] = x1
    h2_scr[...] = _modulated_norm(x1, gffn_ref[...], sh2, sc2).astype(BF16)

    @pl.when(ta % tiles_per_seq == tiles_per_seq - 1)
    def _():
        for j in range(d // LANES):
            cs_ref[:, j * LANES:(j + 1) * LANES] = zbuf[j, tt + CARRY_SKEW:tt + CARRY_ROWS, :]

    zbuf[:, 0:CARRY_ROWS, :] = zbuf[:, tt:tt + CARRY_ROWS, :]


def _l0_sample_kernel(x_ref, mod_ref, ctx_ref, nxt_ref, gmix_ref, gffn_ref, wpw1_ref, bpw1_ref, wdw_ref, bdw_ref,
                      lng_ref, lnb_ref, wpw2_ref, bpw2_ref, wg_ref, wu_ref, wd_ref,
                      xo_ref, cso_ref, z_scr, y_scr):
    j = pl.program_id(0)
    last = pl.num_programs(0) - 1

    @pl.when(j == 0)
    def _():
        sh1, sc1 = _split_mod(mod_ref)[:2]
        hb = _modulated_norm(x_ref[...], gmix_ref[...], sh1, sc1).astype(BF16)
        for col in range(D_MODEL // LANES):
            z_scr[:, col * LANES:(col + 1) * LANES] = _glu_column(hb, wpw1_ref, bpw1_ref, col)
        y_scr[...] = jnp.zeros(y_scr.shape, F32) + bdw_ref[...]

    y_scr[...] += wdw_ref[pl.ds(2 * j, 1), :] * ctx_ref[0] + wdw_ref[pl.ds(2 * j + 1, 1), :] * ctx_ref[1]
    cso_ref[0] = ctx_ref[1]

    @pl.when(j < last)
    def _():
        cso_ref[1] = nxt_ref[...]

    @pl.when(j == last)
    def _():
        z = z_scr[...]
        cso_ref[1] = z
        y = y_scr[...] + wdw_ref[CONV_CTX:CONV_WIDTH, :] * z
        _, _, g1, sh2, sc2, g2 = _split_mod(mod_ref)
        x1 = _conv_out(y, x_ref[...], g1, lng_ref[...], lnb_ref[...], wpw2_ref, bpw2_ref[...])
        hb = _modulated_norm(x1, gffn_ref[...], sh2, sc2).astype(BF16)
        xo_ref[...] = x1 + g2 * _swiglu(hb, wg_ref, wu_ref, wd_ref)


def _gmlp_in(x, sh1, sc1, gmix, win_ref, bin, lng, lnb):
    a = jax.nn.gelu(_dot(_modulated_norm(x, gmix, sh1, sc1), win_ref[...]) + bin)
    return a[:, :GMLP_WIDTH], _layernorm(a[:, GMLP_WIDTH:], lng, lnb)


def _l1_prompt_kernel(x_ref, mod_ref, gmix_ref, gffn_ref, gfin_ref, win_ref, bin_ref, lng_ref, lnb_ref,
                      ws_ref, bsf_ref, wout_ref, bout_ref, wg_ref, wu_ref, wd_ref,
                      yo_ref, um_scr, x1_scr, h2_scr, *, n_tiles, tiles_per_seq):
    c, gd = GMLP_CHUNK, GMLP_GROUP_DIM
    tt = x_ref.shape[0]
    s, _, seq_a, seq_b = _prompt_steps(n_tiles, tiles_per_seq)

    @pl.when(s == 0)
    def _():
        x1_scr[...] = jnp.zeros(x1_scr.shape, F32)
        h2_scr[...] = jnp.zeros(h2_scr.shape, BF16)

    hb_prev = h2_scr[...]
    gate = jnp.dot(hb_prev, wg_ref[...], preferred_element_type=F32)

    x = x_ref[...]
    sh1, sc1, g1, sh2, sc2, _ = _split_mod(mod_ref, seq_a)
    u, v = _gmlp_in(x, sh1, sc1, gmix_ref[...], win_ref, bin_ref[...], lng_ref[...], lnb_ref[...])
    vb = v.astype(BF16)

    g2_b = _split_mod(mod_ref, seq_b)[5]
    up = jnp.dot(hb_prev, wu_ref[...], preferred_element_type=F32)
    x2 = x1_scr[...] + g2_b * _dot(_silu(gate) * up, wd_ref[...])
    yo_ref[...] = _rmsnorm(x2, gfin_ref[...])

    row = jax.lax.broadcasted_iota(jnp.int32, (c, c), 0)
    col = jax.lax.broadcasted_iota(jnp.int32, (c, c), 1)
    causal = row >= col
    for g in range(GMLP_GROUPS):
        cols = slice(g * gd, (g + 1) * gd)
        wt = jnp.where(causal, ws_ref[g], 0.0).astype(BF16)
        vg = jnp.concatenate([vb[r0:r0 + c, cols] for r0 in range(0, tt, c)], axis=1)
        m = jnp.dot(wt, vg, preferred_element_type=F32)
        for i, r0 in enumerate(range(0, tt, c)):
            mi = m[:, i * gd:(i + 1) * gd] + bsf_ref[:, cols]
            um_scr[r0:r0 + c, cols] = (u[r0:r0 + c, cols] * mi).astype(BF16)

    x1 = x + g1 * (jnp.dot(um_scr[...], wout_ref[...], preferred_element_type=F32) + bout_ref[...])
    x1_scr[...] = x1
    h2_scr[...] = _modulated_norm(x1, gffn_ref[...], sh2, sc2).astype(BF16)


def _l1_sample_kernel(x_ref, mod_ref, gmix_ref, gffn_ref, gfin_ref, win_ref, bin_ref, lng_ref, lnb_ref,
                      w00_ref, b0_ref, wout_ref, bout_ref, wg_ref, wu_ref, wd_ref, yo_ref, vo_ref):
    x = x_ref[...]
    sh1, sc1, g1, sh2, sc2, g2 = _split_mod(mod_ref)
    u, v = _gmlp_in(x, sh1, sc1, gmix_ref[...], win_ref, bin_ref[...], lng_ref[...], lnb_ref[...])
    vo_ref[...] = v
    um = u * (w00_ref[...] * v + b0_ref[...])
    x1 = x + g1 * (_dot(um, wout_ref[...]) + bout_ref[...])
    hb = _modulated_norm(x1, gffn_ref[...], sh2, sc2).astype(BF16)
    x2 = x1 + g2 * _swiglu(hb, wg_ref, wu_ref, wd_ref)
    yo_ref[...] = _rmsnorm(x2, gfin_ref[...])


class _Layer(NamedTuple):
    stacked: jax.Array
    layer: int


def _resident(c):
    if isinstance(c, _Layer):
        index = (c.layer,) + (0,) * (c.stacked.ndim - 1)
        return pl.BlockSpec((None,) + c.stacked.shape[1:], lambda *_: index, pipeline_mode=pl.Buffered(1))
    index = (0,) * c.ndim
    return pl.BlockSpec(c.shape, lambda *_: index, pipeline_mode=pl.Buffered(1))


def _operand(c):
    return c.stacked if isinstance(c, _Layer) else c


def _row(v):
    return v.reshape(1, -1)


def _params(n_grid):
    return pltpu.CompilerParams(dimension_semantics=("arbitrary",) * n_grid,
                                vmem_limit_bytes=VMEM_LIMIT_BYTES)


def _prompt_tiling(x, tt):
    b, t, d = x.shape
    assert t % tt == 0 and tt % GMLP_CHUNK == 0 and tt >= CARRY_ROWS
    n_tiles = b * t // tt
    mixer_tile = pl.BlockSpec((tt, d), lambda s: (jnp.minimum(s, n_tiles - 1), 0))
    ffn_tile = pl.BlockSpec((tt, d), lambda s: (jnp.maximum(s - 1, 0), 0))
    return x.reshape(b * t, d), mixer_tile, ffn_tile, dict(n_tiles=n_tiles, tiles_per_seq=t // tt)


def _cast_blocks(w, n_steps):
    rows, cols = w.stacked.shape[1:]
    blk = next(r for r in range(BF16_SUBLANES, rows + 1, BF16_SUBLANES) if rows % r == 0 and rows // r <= n_steps)
    last = rows // blk - 1
    src = pl.BlockSpec((None, blk, cols), lambda s: (w.layer, jnp.minimum(s, last), 0))
    dst = pl.BlockSpec((blk, cols), lambda s: (jnp.minimum(s, last), 0))
    return src, dst, jax.ShapeDtypeStruct((rows, cols), BF16)


def _l0_prompt_call(x, mod, consts, cast):
    b, t, d = x.shape
    tt = L0_PROMPT_TILE
    x2d, mixer_tile, ffn_tile, counts = _prompt_tiling(x, tt)
    n_tiles, tiles_per_seq = counts["n_tiles"], counts["tiles_per_seq"]
    assert len(consts) == N_L0_CONSTS
    state = pl.BlockSpec((None, None, CONV_CTX, d),
                         lambda s: (0, jnp.minimum(s, n_tiles - 1) // tiles_per_seq, 0, 0))
    cast_src, cast_dst, cast_shapes = zip(*[_cast_blocks(w, n_tiles + 1) for w in cast])
    xo, cs, *cast_out = pl.pallas_call(
        functools.partial(_l0_prompt_kernel, n_cast=len(cast), **counts),
        grid=(n_tiles + 1,),
        in_specs=[mixer_tile, _resident(mod)] + [_resident(c) for c in consts] + list(cast_src),
        out_specs=[ffn_tile, state] + list(cast_dst),
        out_shape=[jax.ShapeDtypeStruct((b * t, d), F32), jax.ShapeDtypeStruct((1, b, CONV_CTX, d), F32)]
        + list(cast_shapes),
        scratch_shapes=[pltpu.VMEM((d // LANES, CARRY_ROWS + tt, LANES), F32),
                        pltpu.VMEM((ZSH_SLOTS, SUBLANES - 1, CARRY_ROWS + tt - SUBLANES, LANES), F32),
                        pltpu.VMEM((tt, d), F32),
                        pltpu.VMEM((tt, d), F32),
                        pltpu.VMEM((tt, d), BF16)],
        compiler_params=_params(1),
        name="l0_prompt",
    )(x2d, _operand(mod), *map(_operand, consts), *map(_operand, cast))
    return xo.reshape(b, t, d), cs, cast_out


def _l0_sample_call(x, mod, ctx_rows, consts):
    n, d = x.shape
    assert CONV_CTX % SAMPLE_TAPS_PER_STEP == 0
    pair = pl.BlockSpec((None, SAMPLE_TAPS_PER_STEP, n, d), lambda j: (0, j, 0, 0))
    nxt = pl.BlockSpec((None, None, n, d),
                       lambda j: (0, jnp.minimum(SAMPLE_TAPS_PER_STEP * (j + 1), CONV_CTX - 1), 0, 0))
    return pl.pallas_call(
        _l0_sample_kernel,
        grid=(CONV_CTX // SAMPLE_TAPS_PER_STEP,),
        in_specs=[_resident(x), _resident(mod), pair, nxt] + [_resident(c) for c in consts],
        out_specs=[pl.BlockSpec((n, d), lambda j: (0, 0)), pair],
        out_shape=[jax.ShapeDtypeStruct((n, d), F32), jax.ShapeDtypeStruct(ctx_rows.shape, F32)],
        scratch_shapes=[pltpu.VMEM((n, d), F32), pltpu.VMEM((n, d), F32)],
        compiler_params=_params(1),
        name="l0_sample",
    )(x, _operand(mod), ctx_rows, ctx_rows, *map(_operand, consts))


def _l1_prompt_call(x, mod, consts):
    b, t, d = x.shape
    tt = L1_PROMPT_TILE
    x2d, mixer_tile, ffn_tile, counts = _prompt_tiling(x, tt)
    y = pl.pallas_call(
        functools.partial(_l1_prompt_kernel, **counts),
        grid=(counts["n_tiles"] + 1,),
        in_specs=[mixer_tile, _resident(mod)] + [_resident(c) for c in consts],
        out_specs=ffn_tile,
        out_shape=jax.ShapeDtypeStruct((b * t, d), F32),
        scratch_shapes=[pltpu.VMEM((tt, GMLP_WIDTH), BF16),
                        pltpu.VMEM((tt, d), F32),
                        pltpu.VMEM((tt, d), BF16)],
        compiler_params=_params(1),
        name="l1_prompt",
    )(x2d, _operand(mod), *map(_operand, consts))
    return y.reshape(b, t, d)


def _l1_sample_call(x, mod, consts):
    n, d = x.shape
    return pl.pallas_call(
        _l1_sample_kernel,
        grid=(1,),
        in_specs=[_resident(x), _resident(mod)] + [_resident(c) for c in consts],
        out_specs=[pl.BlockSpec((n, d), lambda i: (0, 0)), pl.BlockSpec((n, GMLP_WIDTH), lambda i: (0, 0))],
        out_shape=[jax.ShapeDtypeStruct((n, d), F32), jax.ShapeDtypeStruct((n, GMLP_WIDTH), F32)],
        compiler_params=_params(1),
        name="l1_sample",
    )(x, _operand(mod), *map(_operand, consts))


def kernel(x_prompt, x_sample, c_prompt, c_sample, state_conv, w_ada, b_ada, norm_mix_g, norm_ffn_g, final_norm_g, conv_w_pw1, conv_b_pw1, conv_w_dw, conv_b_dw, conv_ln_g, conv_ln_b, conv_w_pw2, conv_b_pw2, gmlp_w_in, gmlp_b_in, gmlp_ln_g, gmlp_ln_b, gmlp_w_s, gmlp_b_s, gmlp_w_out, gmlp_b_out, ffn_w_gate, ffn_w_up, ffn_w_down):
    d = x_prompt.shape[-1]
    ns = x_sample.shape[0]
    assert x_sample.shape[1] == 1 and state_conv.shape[0] == 1 and gmlp_w_in.shape[0] == 1

    mod_p, mod_s = _ada_call(c_prompt, c_sample, w_ada, b_ada)

    l0 = (_row(norm_mix_g[0]), _row(norm_ffn_g[0]), _Layer(conv_w_pw1.astype(BF16), 0), _row(conv_b_pw1[0]),
          _Layer(conv_w_dw, 0), _row(conv_b_dw[0]), _row(conv_ln_g[0]), _row(conv_ln_b[0]),
          _Layer(conv_w_pw2.astype(BF16), 0), _row(conv_b_pw2[0]),
          ffn_w_gate[0].astype(BF16), ffn_w_up[0].astype(BF16), ffn_w_down[0].astype(BF16))

    l1_f32 = (_Layer(gmlp_w_in, 0), _Layer(gmlp_w_out, 0),
              _Layer(ffn_w_gate, 1), _Layer(ffn_w_up, 1), _Layer(ffn_w_down, 1))
    xp, cs_p, (w_in, w_out, w_gate, w_up, w_down) = _l0_prompt_call(x_prompt, _Layer(mod_p, 0), l0, l1_f32)

    l1_head = (_row(norm_mix_g[1]), _row(norm_ffn_g[1]), _row(final_norm_g), w_in,
               _row(gmlp_b_in[0]), _row(gmlp_ln_g[0]), _row(gmlp_ln_b[0]))
    l1_tail = (w_out, _row(gmlp_b_out[0]), w_gate, w_up, w_down)

    bias_full = jnp.repeat(gmlp_b_s[0].T, GMLP_GROUP_DIM, axis=1)
    w00 = _row(jnp.repeat(gmlp_w_s[0][:, 0, 0], GMLP_GROUP_DIM))
    b0 = _row(jnp.repeat(gmlp_b_s[0][:, 0], GMLP_GROUP_DIM))

    y_prompt = _l1_prompt_call(xp, _Layer(mod_p, 1), l1_head + (_Layer(gmlp_w_s, 0), bias_full) + l1_tail)

    xs, cs_rows = _l0_sample_call(x_sample.reshape(ns, d), _Layer(mod_s, 0), jnp.swapaxes(state_conv, 1, 2), l0)
    y_sample, v_s = _l1_sample_call(xs, _Layer(mod_s, 1), l1_head + (w00, b0) + l1_tail)

    return (y_prompt, y_sample.reshape(ns, 1, d), cs_p, jnp.swapaxes(cs_rows, 1, 2),
            v_s.reshape(1, ns, 1, GMLP_WIDTH))
```
